```python
import jax, jax.numpy as jnp
from jax import lax
import numpy as np

D_MODEL = 2048
BATCH = 8
SEQ = 2048
DEPTH = 4

HEAD_DIM = 128
ATTN_HEADS_PER_GROUP = 4
DILATED_GROUPS = ((128, 1), (512, 4), (2048, 16))
N_ATTN_GROUPS = len(DILATED_GROUPS)
GROUP_WIDTH = ATTN_HEADS_PER_GROUP * HEAD_DIM
ATTN_WIDTH = N_ATTN_GROUPS * GROUP_WIDTH
ATTN_OUT_WIDTH = GROUP_WIDTH
HGRN_HEADS = 8
HGRN_KEY_DIM = 128
HGRN_VAL_DIM = 128
HGRN_WIDTH = HGRN_HEADS * HGRN_KEY_DIM
HGRN_CHUNK = 64
D_FF = 5504
MACARON_WEIGHT = 0.5
EPS = 1e-6
MASK_VALUE = -1e30
SPLIT_SIZES = (ATTN_WIDTH, ATTN_WIDTH, ATTN_WIDTH,
               HGRN_WIDTH, HGRN_WIDTH, HGRN_HEADS * HGRN_VAL_DIM, HGRN_HEADS * HGRN_VAL_DIM,
               2 * D_MODEL)
IN_WIDTH = sum(SPLIT_SIZES)

kernel_name = "hybrid_dilated_attn_hgrn2_macaron"


def rms_norm(x, gain):
    xf = x.astype(jnp.float32)
    y = xf * lax.rsqrt(jnp.mean(xf * xf, axis=-1, keepdims=True) + EPS)
    return (y * gain.astype(jnp.float32)).astype(x.dtype)


def swiglu(h, w_in, w_out):
    gate, up = jnp.split(h @ w_in, 2, axis=-1)
    return (jax.nn.silu(gate) * up) @ w_out


def dilated_window_attention(q, k, v, window, dilation):
    B, S, H, Dh = q.shape
    r = dilation
    nb = window // dilation
    L = S // r
    nblk = -(-L // nb)
    pad = nblk * nb - L

    def to_blocks(t):
        t = t.reshape(B, L, r, H, Dh).transpose(0, 2, 3, 1, 4)
        t = jnp.pad(t, ((0, 0), (0, 0), (0, 0), (0, pad), (0, 0)))
        return t.reshape(B, r, H, nblk, nb, Dh)

    def with_prev(t):
        prev = jnp.concatenate([jnp.zeros_like(t[:, :, :, :1]), t[:, :, :, :-1]], axis=3)
        return jnp.concatenate([prev, t], axis=4)

    qb = to_blocks(q)
    kb = with_prev(to_blocks(k))
    vb = with_prev(to_blocks(v))
    s = jnp.einsum('brhnqd,brhnkd->brhnqk', qb, kb).astype(jnp.float32) * (Dh ** -0.5)
    qi = jnp.arange(nb)[:, None]
    kj = jnp.arange(2 * nb)[None, :]
    dist = nb + qi - kj
    blk = jnp.arange(nblk)[:, None, None]
    valid = (dist >= 0) & (dist <= nb) & ((blk > 0) | (kj >= nb))
    s = jnp.where(valid, s, MASK_VALUE)
    lse = jax.nn.logsumexp(s, axis=-1)
    p = jnp.exp(s - lse[..., None])
    o = jnp.einsum('brhnqk,brhnkd->brhnqd', p.astype(vb.dtype), vb)
    o = o.reshape(B, r, H, nblk * nb, Dh)[:, :, :, :L].transpose(0, 3, 1, 2, 4).reshape(B, S, H, Dh)
    lse = lse.reshape(B, r, H, nblk * nb)[..., :L].transpose(0, 3, 1, 2).reshape(B, S, H)
    return o, lse


def dilated_attention_branch(q, k, v):
    B, S, _ = q.shape
    outs, lses = [], []
    for g, (window, dilation) in enumerate(DILATED_GROUPS):
        sl = slice(g * GROUP_WIDTH, (g + 1) * GROUP_WIDTH)
        qg, kg, vg = (t[..., sl].reshape(B, S, ATTN_HEADS_PER_GROUP, HEAD_DIM) for t in (q, k, v))
        o, lse = dilated_window_attention(qg, kg, vg, window, dilation)
        outs.append(o.astype(jnp.float32))
        lses.append(lse)
    weights = jax.nn.softmax(jnp.stack(lses), axis=0)
    out = jnp.einsum('gbsh,gbshd->bshd', weights, jnp.stack(outs))
    return out.reshape(B, S, ATTN_OUT_WIDTH).astype(q.dtype)


def hgrn2_branch(q, f_logit, i, g_out, lower_bound, norm_gain):
    B, S, _ = q.shape
    H, K, V, C = HGRN_HEADS, HGRN_KEY_DIM, HGRN_VAL_DIM, HGRN_CHUNK
    N = S // C
    dt = q.dtype
    lb = lower_bound.astype(jnp.float32)
    fl = f_logit.astype(jnp.float32)
    log_f = jnp.log(lb + (1.0 - lb) * jax.nn.sigmoid(fl))
    key = (1.0 - lb) * jax.nn.sigmoid(-fl)
    qf = jax.nn.silu(q.astype(jnp.float32))
    vf = i.astype(jnp.float32)

    def chunked(t, w):
        return t.reshape(B, N, C, H, w).transpose(1, 0, 3, 2, 4)

    causal = jnp.tril(jnp.ones((C, C), dtype=bool))

    def step(state, inp):
        qc, kc, vc, lfc = inp
        b = jnp.cumsum(lfc, axis=2)
        o_inter = jnp.einsum('bhtk,bhkv->bhtv', qc * jnp.exp(b), state)
        diff = b[:, :, :, None, :] - b[:, :, None, :, :]
        decay = jnp.exp(jnp.where(causal[:, :, None], diff, MASK_VALUE))
        scores = jnp.einsum('bhtk,bhsk,bhtsk->bhts', qc, kc, decay)
        o = o_inter + jnp.einsum('bhts,bhsv->bhtv', scores, vc)
        b_last = b[:, :, -1:, :]
        state = (jnp.exp(b_last[:, :, 0, :])[..., None] * state
                 + jnp.einsum('bhsk,bhsv->bhkv', kc * jnp.exp(b_last - b), vc))
        return state, o

    s0 = jnp.zeros((B, H, K, V), jnp.float32)
    _, o = lax.scan(step, s0, (chunked(qf, K), chunked(key, K), chunked(vf, V), chunked(log_f, K)))
    o = o.transpose(1, 0, 3, 2, 4).reshape(B, S, H, V)
    o = o * lax.rsqrt(jnp.mean(o * o, axis=-1, keepdims=True) + EPS) * norm_gain.astype(jnp.float32)
    o = o.reshape(B, S, H * V) * jax.nn.silu(g_out.astype(jnp.float32))
    return o.astype(dt)


def setup_inputs(seed: int = 0) -> dict:
    key = jax.random.key(seed)
    ks = jax.random.split(key, 14)
    nrm = lambda k, shape, fan_in: jax.random.normal(k, shape, jnp.float32) * (fan_in ** -0.5)
    gain = lambda k, shape: 1.0 + 0.02 * jax.random.normal(k, shape, jnp.float32)
    HV = HGRN_HEADS * HGRN_VAL_DIM
    return {
        "x": jax.random.normal(ks[0], (BATCH, SEQ, D_MODEL), jnp.float32),
        "ffn_norm": gain(ks[1], (DEPTH, 2, D_MODEL)),
        "ffn_w_in": nrm(ks[2], (DEPTH, 2, D_MODEL, 2 * D_FF), D_MODEL),
        "ffn_w_out": nrm(ks[3], (DEPTH, 2, D_FF, D_MODEL), D_FF),
        "mix_norm": gain(ks[4], (DEPTH, D_MODEL)),
        "w_in": nrm(ks[5], (DEPTH, D_MODEL, IN_WIDTH), D_MODEL),
        "b_gate": 0.01 * jax.random.normal(ks[6], (DEPTH, 2 * D_MODEL), jnp.float32),
        "hgrn_lb": 1.0 + 0.1 * jax.random.normal(ks[7], (DEPTH, HGRN_WIDTH), jnp.float32),
        "hgrn_norm": gain(ks[8], (DEPTH, HGRN_VAL_DIM)),
        "w_proj_attn": nrm(ks[9], (DEPTH, ATTN_OUT_WIDTH, D_MODEL), ATTN_OUT_WIDTH),
        "w_proj_hgrn": nrm(ks[10], (DEPTH, HV, D_MODEL), HV),
        "w_out": nrm(ks[11], (DEPTH, D_MODEL, D_MODEL), D_MODEL),
        "final_norm": gain(ks[12], (D_MODEL,)),
    }


def reference(x, ffn_norm, ffn_w_in, ffn_w_out, mix_norm, w_in, b_gate, hgrn_lb, hgrn_norm,
              w_proj_attn, w_proj_hgrn, w_out, final_norm):
    p = jax.nn.softmax(hgrn_lb.astype(jnp.float32), axis=0)
    lower_bounds = jnp.cumsum(p, axis=0) - p[0:1]
    split_idx = np.cumsum(SPLIT_SIZES)[:-1].tolist()
    for l in range(DEPTH):
        x = x + MACARON_WEIGHT * swiglu(rms_norm(x, ffn_norm[l, 0]), ffn_w_in[l, 0], ffn_w_out[l, 0])
        h = rms_norm(x, mix_norm[l])
        z = h @ w_in[l]
        q_a, k_a, v_a, q_h, f_h, i_h, g_h, gate_logits = jnp.split(z, split_idx, axis=-1)
        a = dilated_attention_branch(q_a, k_a, v_a) @ w_proj_attn[l]
        m = hgrn2_branch(q_h, f_h, i_h, g_h, lower_bounds[l], hgrn_norm[l]) @ w_proj_hgrn[l]
        gate_a, gate_m = jnp.split(jax.nn.sigmoid(gate_logits + b_gate[l]), 2, axis=-1)
        x = x + (gate_a * a + gate_m * m) @ w_out[l]
        x = x + MACARON_WEIGHT * swiglu(rms_norm(x, ffn_norm[l, 1]), ffn_w_in[l, 1], ffn_w_out[l, 1])
    return rms_norm(x, final_norm)
```

```python
import functools

import numpy as np
import jax
import jax.numpy as jnp
from jax import lax
from jax.experimental import pallas as pl
from jax.experimental.pallas import tpu as pltpu

F32 = jnp.float32
BF16 = jnp.bfloat16

D_MODEL = 2048
SEQ = 2048
DEPTH = 4
HEAD_DIM = 128
ATTN_HEADS = 4
DILATED_GROUPS = ((128, 1), (512, 4), (2048, 16))
GROUP_WIDTH = ATTN_HEADS * HEAD_DIM
ATTN_WIDTH = len(DILATED_GROUPS) * GROUP_WIDTH
HGRN_HEADS = 8
HGRN_DIM = 128
HGRN_WIDTH = HGRN_HEADS * HGRN_DIM
D_FF = 5504
D_FF_PAD = 5632
MACARON_WEIGHT = 0.5
EPS = 1e-6
MASK_VALUE = -1e30
LANES = 128
ATTN_BLOCK = 128
HGRN_CHUNK = 64
HGRN_LEVELS = (1, 2, 4, 8, 16, 32)
MIB = 1024 * 1024


def _rms_normalize(x, gain):
    ms = jnp.mean(x * x, axis=-1, keepdims=True)
    return x * lax.rsqrt(ms + EPS) * gain


def _dot(a, b):
    return jnp.dot(a, b, preferred_element_type=F32)


def _dot_nt(a, b):
    return lax.dot_general(a, b, (((1,), (1,)), ((), ())), preferred_element_type=F32)


def _dot_tn(a, b):
    return lax.dot_general(a, b, (((0,), (0,)), ((), ())), preferred_element_type=F32)


def _params(semantics, vmem_mib):
    return pltpu.CompilerParams(dimension_semantics=semantics, vmem_limit_bytes=vmem_mib * MIB)


def _ffn_kernel(x_ref, gain_ref, wg_ref, wu_ref, wo_ref, o_ref, h_ref):
    @pl.when(pl.program_id(1) == 0)
    def _():
        x = x_ref[...]
        h_ref[...] = _rms_normalize(x, gain_ref[...]).astype(BF16)
        o_ref[...] = x

    h = h_ref[...]
    g = _dot(h, wg_ref[...])
    u = _dot(h, wu_ref[...])
    a = (MACARON_WEIGHT * g * jax.nn.sigmoid(g)) * u
    o_ref[...] += _dot(a.astype(BF16), wo_ref[...])


def _ffn(x, gain, w_in_p, w_out_p, layer, which, *, tm=512, tf=512):
    t, d = x.shape
    nf = D_FF_PAD // tf
    return pl.pallas_call(
        _ffn_kernel,
        grid=(t // tm, nf),
        in_specs=[
            pl.BlockSpec((tm, d), lambda i, j: (i, 0)),
            pl.BlockSpec((None, None, 1, d), lambda i, j: (layer, which, 0, 0)),
            pl.BlockSpec((None, None, d, tf), lambda i, j: (layer, which, 0, j)),
            pl.BlockSpec((None, None, d, tf), lambda i, j: (layer, which, 0, nf + j)),
            pl.BlockSpec((None, None, tf, d), lambda i, j: (layer, which, j, 0)),
        ],
        out_specs=pl.BlockSpec((tm, d), lambda i, j: (i, 0)),
        out_shape=jax.ShapeDtypeStruct((t, d), F32),
        scratch_shapes=[pltpu.VMEM((tm, d), BF16)],
        compiler_params=_params(("parallel", "arbitrary"), 48),
        name="ffn",
    )(x, gain, w_in_p, w_in_p, w_out_p)


def _norm_matmul_kernel(x_ref, gain_ref, w_ref, o_ref, h_ref):
    @pl.when(pl.program_id(1) == 0)
    def _():
        h_ref[...] = _rms_normalize(x_ref[...], gain_ref[...]).astype(BF16)

    o_ref[...] = _dot(h_ref[...], w_ref[...]).astype(o_ref.dtype)


def _norm_matmul(x, gain, w, layer, col_block0, n_cols, *, tm=1024, tn=512):
    t, d = x.shape
    return pl.pallas_call(
        _norm_matmul_kernel,
        grid=(t // tm, n_cols // tn),
        in_specs=[
            pl.BlockSpec((tm, d), lambda i, j: (i, 0)),
            pl.BlockSpec((None, 1, d), lambda i, j: (layer, 0, 0)),
            pl.BlockSpec((None, d, tn), lambda i, j: (layer, 0, col_block0 + j)),
        ],
        out_specs=pl.BlockSpec((tm, tn), lambda i, j: (i, j)),
        out_shape=jax.ShapeDtypeStruct((t, n_cols), BF16),
        scratch_shapes=[pltpu.VMEM((tm, d), BF16)],
        compiler_params=_params(("parallel", "arbitrary"), 40),
        name="in_proj",
    )(x, gain, w)


def _attn_kernel(q0, k0, v0, q1, k1, v1, q2, k2, v2, o_ref,
                 qs, ks, vs, og0, og1, og2, lg0, lg1, lg2):
    nb = ATTN_BLOCK
    scale = HEAD_DIM ** -0.5
    row2 = lax.broadcasted_iota(jnp.int32, (nb, 2 * nb), 0)
    col2 = lax.broadcasted_iota(jnp.int32, (nb, 2 * nb), 1)
    band = (col2 >= row2) & (col2 <= row2 + nb)
    row1 = lax.broadcasted_iota(jnp.int32, (nb, nb), 0)
    col1 = lax.broadcasted_iota(jnp.int32, (nb, nb), 1)
    causal = col1 <= row1

    def block(q, k, v, mask):
        s = _dot_nt(q, k) * scale
        s = jnp.where(mask, s, MASK_VALUE)
        m = jnp.max(s, axis=-1, keepdims=True)
        p = jnp.exp(s - m)
        l = jnp.sum(p, axis=-1, keepdims=True)
        o = _dot(p.astype(BF16), v)
        return o / l, m + jnp.log(l)

    def run_group(dilation, q_ref, k_ref, v_ref, og, lg):
        r = dilation
        nblk = SEQ // (r * nb)
        if r > 1:
            qs[...] = q_ref[...].astype(F32)
            ks[...] = k_ref[...].astype(F32)
            vs[...] = v_ref[...].astype(F32)
        for c in range(r):
            for n in range(nblk):
                first = n == 0
                nk = nb if first else 2 * nb
                q_start = c + r * nb * n
                k_start = q_start if first else q_start - r * nb
                if r == 1:
                    q = q_ref[pl.ds(q_start, nb), :]
                    k = k_ref[pl.ds(k_start, nk), :]
                    v = v_ref[pl.ds(k_start, nk), :]
                    rows = pl.ds(q_start, nb)
                else:
                    q = qs[pl.ds(q_start, nb, stride=r), :].astype(BF16)
                    k = ks[pl.ds(k_start, nk, stride=r), :].astype(BF16)
                    v = vs[pl.ds(k_start, nk, stride=r), :].astype(BF16)
                    rows = pl.ds(q_start, nb, stride=r)
                o, lse = block(q, k, v, causal if first else band)
                og[rows, :] = o
                lg[rows, :] = jnp.broadcast_to(lse, (nb, LANES))

    run_group(DILATED_GROUPS[0][1], q0, k0, v0, og0, lg0)
    run_group(DILATED_GROUPS[1][1], q1, k1, v1, og1, lg1)
    run_group(DILATED_GROUPS[2][1], q2, k2, v2, og2, lg2)

    tile = 256
    for t in range(SEQ // tile):
        rows = pl.ds(t * tile, tile)
        l0, l1, l2 = lg0[rows, :], lg1[rows, :], lg2[rows, :]
        mx = jnp.maximum(jnp.maximum(l0, l1), l2)
        w0, w1, w2 = jnp.exp(l0 - mx), jnp.exp(l1 - mx), jnp.exp(l2 - mx)
        num = w0 * og0[rows, :] + w1 * og1[rows, :] + w2 * og2[rows, :]
        o_ref[rows, :] = (num / (w0 + w1 + w2)).astype(BF16)


def _attention(z_attn):
    b = z_attn.shape[0]
    heads_per_part = ATTN_WIDTH // HEAD_DIM

    def spec(part, group):
        base = part * heads_per_part + group * ATTN_HEADS
        return pl.BlockSpec((None, SEQ, HEAD_DIM), lambda bi, h: (bi, 0, base + h))

    in_specs = [spec(part, group) for group in range(3) for part in range(3)]
    seq_f32 = pltpu.VMEM((SEQ, HEAD_DIM), F32)
    return pl.pallas_call(
        _attn_kernel,
        grid=(b, ATTN_HEADS),
        in_specs=in_specs,
        out_specs=pl.BlockSpec((None, SEQ, HEAD_DIM), lambda bi, h: (bi, 0, h)),
        out_shape=jax.ShapeDtypeStruct((b, SEQ, GROUP_WIDTH), BF16),
        scratch_shapes=[seq_f32] * 9,
        compiler_params=_params(("parallel", "parallel"), 40),
        name="dilated_attn",
    )(*([z_attn] * 9))


def _hgrn_masks():
    c = HGRN_CHUNK
    t = np.arange(c)[:, None]
    s = np.arange(c)[None, :]
    masks = []
    for h in HGRN_LEVELS:
        masks.append((t // (2 * h) == s // (2 * h)) & (t % (2 * h) >= h) & (s % (2 * h) < h))
    masks.append(t == s)
    return np.stack(masks).astype(np.float32)


def _hgrn_kernel(layer, zq, zf, zi, zg, lb_ref, gn_ref, mk_ref, o_ref, st_ref, b_sc):
    c = HGRN_CHUNK
    tc = zq.shape[0]

    @pl.when(pl.program_id(1) == 0)
    def _():
        st_ref[...] = jnp.zeros_like(st_ref)

    lb_all = lb_ref[...]
    e = jnp.exp(lb_all - jnp.max(lb_all, axis=0, keepdims=True))
    p = e / jnp.sum(e, axis=0, keepdims=True)
    lower = jnp.zeros((1, HGRN_WIDTH), F32)
    for i in range(1, layer + 1):
        lower = lower + p[i:i + 1, :]
    gain = gn_ref[layer:layer + 1, :]

    rowi = lax.broadcasted_iota(jnp.int32, (c, HGRN_DIM), 0)

    def chunk(ci, carry):
        r0 = pl.multiple_of(ci * c, c)
        rows = pl.ds(r0, c)
        for hd in range(HGRN_HEADS):
            cols = slice(hd * HGRN_DIM, (hd + 1) * HGRN_DIM)
            lb = lower[:, cols]
            q = zq[rows, cols].astype(F32)
            fl = zf[rows, cols].astype(F32)
            iv = zi[rows, cols]
            sg = jax.nn.sigmoid(fl)
            lf = jnp.log(lb + (1.0 - lb) * sg)
            kk = (1.0 - lb) * (1.0 - sg)
            qf = q * jax.nn.sigmoid(q)

            b = lf
            for d in (1, 2, 4, 8, 16, 32):
                b = b + jnp.where(rowi >= d, pltpu.roll(b, d, axis=0), 0.0)
            b_sc[hd] = b
            b_last = b_sc[hd, c - 1:c, :]

            scores = mk_ref[len(HGRN_LEVELS)] * jnp.sum(qf * kk, axis=-1, keepdims=True)
            for lev, h in enumerate(HGRN_LEVELS):
                if h == 1:
                    g = jnp.where(rowi % 2 == 1, pltpu.roll(b, 1, axis=0), b)
                elif h == 2:
                    r4 = rowi % 4
                    g = jnp.where(r4 == 0, pltpu.roll(b, c - 1, axis=0),
                                  jnp.where(r4 == 1, b,
                                            jnp.where(r4 == 2, pltpu.roll(b, 1, axis=0),
                                                      pltpu.roll(b, 2, axis=0))))
                else:
                    g = jnp.concatenate(
                        [jnp.broadcast_to(b_sc[hd, pl.ds(p0 + h - 1, 1), :], (2 * h, HGRN_DIM))
                         for p0 in range(0, c, 2 * h)], axis=0)
                ed = jnp.exp(-jnp.abs(b - g))
                prod = _dot_nt((qf * ed).astype(BF16), (kk * ed).astype(BF16))
                scores = scores + prod * mk_ref[lev]

            st = st_ref[hd]
            o = _dot_nt((qf * jnp.exp(b)).astype(BF16), st.astype(BF16))
            o = o + _dot(scores.astype(BF16), iv)
            kd = (kk * jnp.exp(b_last - b)).astype(BF16)
            st_ref[hd] = jnp.exp(b_last) * st + _dot_tn(iv, kd)

            gt = zg[rows, cols].astype(F32)
            y = _rms_normalize(o, gain) * (gt * jax.nn.sigmoid(gt))
            o_ref[rows, cols] = y.astype(BF16)
        return carry

    lax.fori_loop(0, tc // c, chunk, 0)


def _hgrn(z_hgrn, hgrn_lb, hgrn_norm, masks, layer, *, tc=512):
    b = z_hgrn.shape[0]

    def spec(part):
        return pl.BlockSpec((None, tc, HGRN_WIDTH), lambda bi, t: (bi, t, part))

    return pl.pallas_call(
        functools.partial(_hgrn_kernel, layer),
        grid=(b, SEQ // tc),
        in_specs=[spec(0), spec(1), spec(2), spec(3),
                  pl.BlockSpec(hgrn_lb.shape, lambda bi, t: (0, 0)),
                  pl.BlockSpec(hgrn_norm.shape, lambda bi, t: (0, 0)),
                  pl.BlockSpec(masks.shape, lambda bi, t: (0, 0, 0))],
        out_specs=pl.BlockSpec((None, tc, HGRN_WIDTH), lambda bi, t: (bi, t, 0)),
        out_shape=jax.ShapeDtypeStruct((b, SEQ, HGRN_WIDTH), BF16),
        scratch_shapes=[pltpu.VMEM((HGRN_HEADS, HGRN_DIM, HGRN_DIM), F32),
                        pltpu.VMEM((HGRN_HEADS, HGRN_CHUNK, HGRN_DIM), F32)],
        compiler_params=_params(("parallel", "arbitrary"), 40),
        name="hgrn2",
    )(z_hgrn, z_hgrn, z_hgrn, z_hgrn, hgrn_lb, hgrn_norm, masks)


def _combine_kernel(x_ref, at_ref, hg_ref, gl_ref, bg_ref, wpa_ref, wpm_ref, wo_ref, o_ref, y_ref):
    d = o_ref.shape[1]
    tn = 512
    at = at_ref[...]
    hg = hg_ref[...]
    for c0 in range(0, d, tn):
        ga = jax.nn.sigmoid(gl_ref[:, c0:c0 + tn].astype(F32) + bg_ref[:, c0:c0 + tn])
        gm = jax.nn.sigmoid(gl_ref[:, d + c0:d + c0 + tn].astype(F32) + bg_ref[:, d + c0:d + c0 + tn])
        y = ga * _dot(at, wpa_ref[:, c0:c0 + tn]) + gm * _dot(hg, wpm_ref[:, c0:c0 + tn])
        y_ref[:, c0:c0 + tn] = y.astype(BF16)
    o_ref[...] = x_ref[...] + _dot(y_ref[...], wo_ref[...])


def _combine(x, attn, hgrn, gate_logits, b_gate, w_pa, w_pm, w_o, layer, *, tm=256):
    t, d = x.shape

    def rows(width):
        return pl.BlockSpec((tm, width), lambda i: (i, 0))

    def whole(a):
        return pl.BlockSpec((None,) + a.shape[1:], lambda i: (layer,) + (0,) * (a.ndim - 1))

    return pl.pallas_call(
        _combine_kernel,
        grid=(t // tm,),
        in_specs=[rows(d), rows(GROUP_WIDTH), rows(HGRN_WIDTH), rows(2 * d),
                  whole(b_gate), whole(w_pa), whole(w_pm), whole(w_o)],
        out_specs=rows(d),
        out_shape=jax.ShapeDtypeStruct((t, d), F32),
        scratch_shapes=[pltpu.VMEM((tm, d), BF16)],
        compiler_params=_params(("parallel",), 56),
        name="merge_out_proj",
    )(x, attn, hgrn, gate_logits, b_gate, w_pa, w_pm, w_o)


def _final_norm_kernel(x_ref, gain_ref, o_ref):
    o_ref[...] = _rms_normalize(x_ref[...], gain_ref[...])


def _final_norm(x, gain, *, tm=512):
    t, d = x.shape
    return pl.pallas_call(
        _final_norm_kernel,
        grid=(t // tm,),
        in_specs=[pl.BlockSpec((tm, d), lambda i: (i, 0)), pl.BlockSpec((1, d), lambda i: (0, 0))],
        out_specs=pl.BlockSpec((tm, d), lambda i: (i, 0)),
        out_shape=jax.ShapeDtypeStruct((t, d), F32),
        compiler_params=_params(("parallel",), 32),
        name="final_norm",
    )(x, gain)


def kernel(x, ffn_norm, ffn_w_in, ffn_w_out, mix_norm, w_in, b_gate, hgrn_lb, hgrn_norm,
           w_proj_attn, w_proj_hgrn, w_out, final_norm):
    bsz, seq, d = x.shape
    assert seq == SEQ and d == D_MODEL
    t = bsz * seq

    pad_cols = jnp.zeros(ffn_w_in.shape[:-1] + (D_FF_PAD - D_FF,), BF16)
    w_in_p = jnp.concatenate([ffn_w_in[..., :D_FF].astype(BF16), pad_cols,
                              ffn_w_in[..., D_FF:].astype(BF16), pad_cols], axis=-1)
    w_out_p = jnp.pad(ffn_w_out.astype(BF16), ((0, 0), (0, 0), (0, D_FF_PAD - D_FF), (0, 0)))
    w_mix = w_in.astype(BF16)
    w_pa = w_proj_attn.astype(BF16)
    w_pm = w_proj_hgrn.astype(BF16)
    w_o = w_out.astype(BF16)
    ffn_gain = ffn_norm.reshape(DEPTH, 2, 1, d)
    mix_gain = mix_norm.reshape(DEPTH, 1, d)
    gate_bias = b_gate.reshape(DEPTH, 1, 2 * d)
    masks = jnp.asarray(_hgrn_masks())

    n_attn = 3 * ATTN_WIDTH
    n_hgrn = 4 * HGRN_WIDTH
    tn = 512
    xf = x.reshape(t, d)
    for l in range(DEPTH):
        xf = _ffn(xf, ffn_gain, w_in_p, w_out_p, l, 0)
        z_attn = _norm_matmul(xf, mix_gain, w_mix, l, 0, n_attn, tn=tn)
        z_hgrn = _norm_matmul(xf, mix_gain, w_mix, l, n_attn // tn, n_hgrn, tn=tn)
        z_gate = _norm_matmul(xf, mix_gain, w_mix, l, (n_attn + n_hgrn) // tn, 2 * d, tn=tn)
        attn = _attention(z_attn.reshape(bsz, seq, n_attn))
        hgrn = _hgrn(z_hgrn.reshape(bsz, seq, n_hgrn), hgrn_lb, hgrn_norm, masks, l)
        xf = _combine(xf, attn.reshape(t, GROUP_WIDTH), hgrn.reshape(t, HGRN_WIDTH), z_gate,
                      gate_bias, w_pa, w_pm, w_o, l)
        xf = _ffn(xf, ffn_gain, w_in_p, w_out_p, l, 1)
    return _final_norm(xf, final_norm.reshape(1, d)).reshape(bsz, seq, d)
```

```python
import functools

import numpy as np
import jax
import jax.numpy as jnp
from jax import lax
from jax.experimental import pallas as pl
from jax.experimental.pallas import tpu as pltpu

F32 = jnp.float32
BF16 = jnp.bfloat16

D_MODEL = 2048
SEQ = 2048
DEPTH = 4
HEAD_DIM = 128
ATTN_HEADS = 4
DILATED_GROUPS = ((128, 1), (512, 4), (2048, 16))
GROUP_WIDTH = ATTN_HEADS * HEAD_DIM
ATTN_WIDTH = len(DILATED_GROUPS) * GROUP_WIDTH
HGRN_HEADS = 8
HGRN_DIM = 128
HGRN_WIDTH = HGRN_HEADS * HGRN_DIM
Z_GATE = 4 * HGRN_WIDTH
Z_ATTN = Z_GATE + 2 * D_MODEL
Z_WIDTH = Z_ATTN + 3 * ATTN_WIDTH
D_FF = 5504
FFN_TILE = 768
FFN_TAIL = 128
MACARON_WEIGHT = 0.5
EPS = 1e-6
MASK_VALUE = -1e30
LANES = 128
ATTN_BLOCK = 128
HGRN_CHUNK = 64
HGRN_LEVELS = (1, 2, 4, 8, 16, 32)
MIB = 1024 * 1024


def _rms_normalize(x, gain):
    ms = jnp.mean(x * x, axis=-1, keepdims=True)
    return x * lax.rsqrt(ms + EPS) * gain


def _dot(a, b):
    return jnp.dot(a, b, preferred_element_type=F32)


def _dot_nt(a, b):
    return lax.dot_general(a, b, (((1,), (1,)), ((), ())), preferred_element_type=F32)


def _dot_tn(a, b):
    return lax.dot_general(a, b, (((0,), (0,)), ((), ())), preferred_element_type=F32)


def _params(semantics, vmem_mib):
    return pltpu.CompilerParams(dimension_semantics=semantics, vmem_limit_bytes=vmem_mib * MIB)


def _swiglu_act(g, u):
    return (MACARON_WEIGHT * g * jax.nn.sigmoid(g)) * u


def _ffn_kernel(x_ref, gain_ref, wg_ref, wu_ref, wo_ref, wgu_tail_ref, wo_tail_ref, o_ref, h_ref):
    @pl.when(pl.program_id(1) == 0)
    def _():
        x = x_ref[...]
        h_ref[...] = _rms_normalize(x, gain_ref[...]).astype(BF16)
        gu = _dot(h_ref[...], wgu_tail_ref[...])
        a = _swiglu_act(gu[:, :FFN_TAIL], gu[:, FFN_TAIL:])
        o_ref[...] = x + _dot(a.astype(BF16), wo_tail_ref[...])

    h = h_ref[...]
    a = _swiglu_act(_dot(h, wg_ref[...]), _dot(h, wu_ref[...]))
    o_ref[...] += _dot(a.astype(BF16), wo_ref[...])


def _ffn(x, gain, w_gu, w_gu_tail, w_down, layer, which, *, tm=512):
    t, d = x.shape
    tf = FFN_TILE
    tail_block = (D_FF - FFN_TAIL) // FFN_TAIL
    return pl.pallas_call(
        _ffn_kernel,
        grid=(t // tm, (D_FF - FFN_TAIL) // tf),
        in_specs=[
            pl.BlockSpec((tm, d), lambda i, j: (i, 0)),
            pl.BlockSpec((None, None, 1, d), lambda i, j: (layer, which, 0, 0)),
            pl.BlockSpec((None, None, None, d, tf), lambda i, j: (layer, which, 0, 0, j)),
            pl.BlockSpec((None, None, None, d, tf), lambda i, j: (layer, which, 1, 0, j)),
            pl.BlockSpec((None, None, tf, d), lambda i, j: (layer, which, j, 0)),
            pl.BlockSpec((None, None, d, 2 * FFN_TAIL), lambda i, j: (layer, which, 0, 0)),
            pl.BlockSpec((None, None, FFN_TAIL, d), lambda i, j: (layer, which, tail_block, 0)),
        ],
        out_specs=pl.BlockSpec((tm, d), lambda i, j: (i, 0)),
        out_shape=jax.ShapeDtypeStruct((t, d), F32),
        scratch_shapes=[pltpu.VMEM((tm, d), BF16)],
        compiler_params=_params(("parallel", "arbitrary"), 52),
        name="ffn",
    )(x, gain, w_gu, w_gu, w_down, w_gu_tail, w_down)


def _norm_matmul_kernel(x_ref, gain_ref, w_ref, o_ref, h_ref):
    @pl.when(pl.program_id(1) == 0)
    def _():
        h_ref[...] = _rms_normalize(x_ref[...], gain_ref[...]).astype(BF16)

    o_ref[...] = _dot(h_ref[...], w_ref[...]).astype(o_ref.dtype)


def _norm_matmul(x, gain, w, layer, *, tm=1024, tn=1280):
    t, d = x.shape
    n_cols = w.shape[-1]
    return pl.pallas_call(
        _norm_matmul_kernel,
        grid=(t // tm, n_cols // tn),
        in_specs=[
            pl.BlockSpec((tm, d), lambda i, j: (i, 0)),
            pl.BlockSpec((None, 1, d), lambda i, j: (layer, 0, 0)),
            pl.BlockSpec((None, d, tn), lambda i, j: (layer, 0, j)),
        ],
        out_specs=pl.BlockSpec((tm, tn), lambda i, j: (i, j)),
        out_shape=jax.ShapeDtypeStruct((t, n_cols), BF16),
        scratch_shapes=[pltpu.VMEM((tm, d), BF16)],
        compiler_params=_params(("parallel", "arbitrary"), 48),
        name="in_proj",
    )(x, gain, w)


def _attn_kernel(q0, k0, v0, q1, k1, v1, q2, k2, v2, o_ref,
                 qs, ks, vs, og0, og1, og2, lg0, lg1, lg2):
    nb = ATTN_BLOCK
    scale = HEAD_DIM ** -0.5
    row2 = lax.broadcasted_iota(jnp.int32, (nb, 2 * nb), 0)
    col2 = lax.broadcasted_iota(jnp.int32, (nb, 2 * nb), 1)
    band = (col2 >= row2) & (col2 <= row2 + nb)
    row1 = lax.broadcasted_iota(jnp.int32, (nb, nb), 0)
    col1 = lax.broadcasted_iota(jnp.int32, (nb, nb), 1)
    causal = col1 <= row1

    def block(q, k, v, mask):
        s = _dot_nt(q, k) * scale
        s = jnp.where(mask, s, MASK_VALUE)
        m = jnp.max(s, axis=-1, keepdims=True)
        p = jnp.exp(s - m)
        l = jnp.sum(p, axis=-1, keepdims=True)
        o = _dot(p.astype(BF16), v)
        return o / l, m + jnp.log(l)

    def run_group(dilation, q_ref, k_ref, v_ref, og, lg):
        r = dilation
        nblk = SEQ // (r * nb)
        if r > 1:
            qs[...] = q_ref[...].astype(F32)
            ks[...] = k_ref[...].astype(F32)
            vs[...] = v_ref[...].astype(F32)
        for c in range(r):
            for n in range(nblk):
                first = n == 0
                nk = nb if first else 2 * nb
                q_start = c + r * nb * n
                k_start = q_start if first else q_start - r * nb
                if r == 1:
                    q = q_ref[pl.ds(q_start, nb), :]
                    k = k_ref[pl.ds(k_start, nk), :]
                    v = v_ref[pl.ds(k_start, nk), :]
                    rows = pl.ds(q_start, nb)
                else:
                    q = qs[pl.ds(q_start, nb, stride=r), :].astype(BF16)
                    k = ks[pl.ds(k_start, nk, stride=r), :].astype(BF16)
                    v = vs[pl.ds(k_start, nk, stride=r), :].astype(BF16)
                    rows = pl.ds(q_start, nb, stride=r)
                o, lse = block(q, k, v, causal if first else band)
                og[rows, :] = o
                lg[rows, :] = jnp.broadcast_to(lse, (nb, LANES))

    run_group(DILATED_GROUPS[0][1], q0, k0, v0, og0, lg0)
    run_group(DILATED_GROUPS[1][1], q1, k1, v1, og1, lg1)
    run_group(DILATED_GROUPS[2][1], q2, k2, v2, og2, lg2)

    tile = 256
    for t in range(SEQ // tile):
        rows = pl.ds(t * tile, tile)
        l0, l1, l2 = lg0[rows, :], lg1[rows, :], lg2[rows, :]
        mx = jnp.maximum(jnp.maximum(l0, l1), l2)
        w0, w1, w2 = jnp.exp(l0 - mx), jnp.exp(l1 - mx), jnp.exp(l2 - mx)
        num = w0 * og0[rows, :] + w1 * og1[rows, :] + w2 * og2[rows, :]
        o_ref[rows, :] = (num / (w0 + w1 + w2)).astype(BF16)


def _attention(z):
    b = z.shape[0]
    heads_per_part = ATTN_WIDTH // HEAD_DIM

    def spec(part, group):
        base = Z_ATTN // HEAD_DIM + part * heads_per_part + group * ATTN_HEADS
        return pl.BlockSpec((None, SEQ, HEAD_DIM), lambda bi, h: (bi, 0, base + h))

    in_specs = [spec(part, group) for group in range(3) for part in range(3)]
    seq_f32 = pltpu.VMEM((SEQ, HEAD_DIM), F32)
    return pl.pallas_call(
        _attn_kernel,
        grid=(b, ATTN_HEADS),
        in_specs=in_specs,
        out_specs=pl.BlockSpec((None, SEQ, HEAD_DIM), lambda bi, h: (bi, 0, h)),
        out_shape=jax.ShapeDtypeStruct((b, SEQ, GROUP_WIDTH), BF16),
        scratch_shapes=[seq_f32] * 9,
        compiler_params=_params(("parallel", "parallel"), 40),
        name="dilated_attn",
    )(*([z] * 9))


def _hgrn_masks():
    c = HGRN_CHUNK
    t = np.arange(c)[:, None]
    s = np.arange(c)[None, :]
    masks = []
    for h in HGRN_LEVELS:
        masks.append((t // (2 * h) == s // (2 * h)) & (t % (2 * h) >= h) & (s % (2 * h) < h))
    masks.append(t == s)
    return np.stack(masks).astype(np.float32)


def _hgrn_kernel(layer, zq, zf, zi, zg, lb_ref, gn_ref, mk_ref, o_ref, st_ref, b_sc):
    c = HGRN_CHUNK
    tc = zq.shape[0]

    @pl.when(pl.program_id(1) == 0)
    def _():
        st_ref[...] = jnp.zeros_like(st_ref)

    lb_all = lb_ref[...]
    e = jnp.exp(lb_all - jnp.max(lb_all, axis=0, keepdims=True))
    p = e / jnp.sum(e, axis=0, keepdims=True)
    lower = jnp.zeros((1, HGRN_WIDTH), F32)
    for i in range(1, layer + 1):
        lower = lower + p[i:i + 1, :]
    gain = gn_ref[layer:layer + 1, :]

    rowi = lax.broadcasted_iota(jnp.int32, (c, HGRN_DIM), 0)

    def chunk(ci, carry):
        r0 = pl.multiple_of(ci * c, c)
        rows = pl.ds(r0, c)
        for hd in range(HGRN_HEADS):
            cols = slice(hd * HGRN_DIM, (hd + 1) * HGRN_DIM)
            lb = lower[:, cols]
            q = zq[rows, cols].astype(F32)
            fl = zf[rows, cols].astype(F32)
            iv = zi[rows, cols]
            sg = jax.nn.sigmoid(fl)
            lf = jnp.log(lb + (1.0 - lb) * sg)
            kk = (1.0 - lb) * (1.0 - sg)
            qf = q * jax.nn.sigmoid(q)

            b = lf
            for d in (1, 2, 4, 8, 16, 32):
                b = b + jnp.where(rowi >= d, pltpu.roll(b, d, axis=0), 0.0)
            b_sc[hd] = b
            b_last = b_sc[hd, c - 1:c, :]

            scores = mk_ref[len(HGRN_LEVELS)] * jnp.sum(qf * kk, axis=-1, keepdims=True)
            for lev, h in enumerate(HGRN_LEVELS):
                if h == 1:
                    g = jnp.where(rowi % 2 == 1, pltpu.roll(b, 1, axis=0), b)
                elif h == 2:
                    r4 = rowi % 4
                    g = jnp.where(r4 == 0, pltpu.roll(b, c - 1, axis=0),
                                  jnp.where(r4 == 1, b,
                                            jnp.where(r4 == 2, pltpu.roll(b, 1, axis=0),
                                                      pltpu.roll(b, 2, axis=0))))
                else:
                    g = jnp.concatenate(
                        [jnp.broadcast_to(b_sc[hd, pl.ds(p0 + h - 1, 1), :], (2 * h, HGRN_DIM))
                         for p0 in range(0, c, 2 * h)], axis=0)
                ed = jnp.exp(-jnp.abs(b - g))
                prod = _dot_nt((qf * ed).astype(BF16), (kk * ed).astype(BF16))
                scores = scores + prod * mk_ref[lev]

            st = st_ref[hd]
            o = _dot_nt((qf * jnp.exp(b)).astype(BF16), st.astype(BF16))
            o = o + _dot(scores.astype(BF16), iv)
            kd = (kk * jnp.exp(b_last - b)).astype(BF16)
            st_ref[hd] = jnp.exp(b_last) * st + _dot_tn(iv, kd)

            gt = zg[rows, cols].astype(F32)
            y = _rms_normalize(o, gain) * (gt * jax.nn.sigmoid(gt))
            o_ref[rows, cols] = y.astype(BF16)
        return carry

    lax.fori_loop(0, tc // c, chunk, 0)


def _hgrn(z, hgrn_lb, hgrn_norm, masks, layer, *, tc=512):
    b = z.shape[0]

    def spec(part):
        return pl.BlockSpec((None, tc, HGRN_WIDTH), lambda bi, t: (bi, t, part))

    return pl.pallas_call(
        functools.partial(_hgrn_kernel, layer),
        grid=(b, SEQ // tc),
        in_specs=[spec(0), spec(1), spec(2), spec(3),
                  pl.BlockSpec(hgrn_lb.shape, lambda bi, t: (0, 0)),
                  pl.BlockSpec(hgrn_norm.shape, lambda bi, t: (0, 0)),
                  pl.BlockSpec(masks.shape, lambda bi, t: (0, 0, 0))],
        out_specs=pl.BlockSpec((None, tc, HGRN_WIDTH), lambda bi, t: (bi, t, 0)),
        out_shape=jax.ShapeDtypeStruct((b, SEQ, HGRN_WIDTH), BF16),
        scratch_shapes=[pltpu.VMEM((HGRN_HEADS, HGRN_DIM, HGRN_DIM), F32),
                        pltpu.VMEM((HGRN_HEADS, HGRN_CHUNK, HGRN_DIM), F32)],
        compiler_params=_params(("parallel", "arbitrary"), 40),
        name="hgrn2",
    )(z, z, z, z, hgrn_lb, hgrn_norm, masks)


def _combine_kernel(x_ref, at_ref, hg_ref, gl_ref, bg_ref, wpa_ref, wpm_ref, wo_ref, o_ref, y_ref):
    d = o_ref.shape[1]
    tn = 512
    at = at_ref[...]
    hg = hg_ref[...]
    for c0 in range(0, d, tn):
        ga = jax.nn.sigmoid(gl_ref[:, c0:c0 + tn].astype(F32) + bg_ref[:, c0:c0 + tn])
        gm = jax.nn.sigmoid(gl_ref[:, d + c0:d + c0 + tn].astype(F32) + bg_ref[:, d + c0:d + c0 + tn])
        y = ga * _dot(at, wpa_ref[:, c0:c0 + tn]) + gm * _dot(hg, wpm_ref[:, c0:c0 + tn])
        y_ref[:, c0:c0 + tn] = y.astype(BF16)
    o_ref[...] = x_ref[...] + _dot(y_ref[...], wo_ref[...])


def _combine(x, attn, hgrn, z, b_gate, w_pa, w_pm, w_o, layer, *, tm=256):
    t, d = x.shape

    def rows(width, col_block=0):
        return pl.BlockSpec((tm, width), lambda i: (i, col_block))

    def whole(a):
        return pl.BlockSpec((None,) + a.shape[1:], lambda i: (layer,) + (0,) * (a.ndim - 1))

    return pl.pallas_call(
        _combine_kernel,
        grid=(t // tm,),
        in_specs=[rows(d), rows(GROUP_WIDTH), rows(HGRN_WIDTH), rows(2 * d, Z_GATE // (2 * d)),
                  whole(b_gate), whole(w_pa), whole(w_pm), whole(w_o)],
        out_specs=rows(d),
        out_shape=jax.ShapeDtypeStruct((t, d), F32),
        scratch_shapes=[pltpu.VMEM((tm, d), BF16)],
        compiler_params=_params(("parallel",), 56),
        name="merge_out_proj",
    )(x, attn, hgrn, z, b_gate, w_pa, w_pm, w_o)


def _final_norm_kernel(x_ref, gain_ref, o_ref):
    o_ref[...] = _rms_normalize(x_ref[...], gain_ref[...])


def _final_norm(x, gain, *, tm=512):
    t, d = x.shape
    return pl.pallas_call(
        _final_norm_kernel,
        grid=(t // tm,),
        in_specs=[pl.BlockSpec((tm, d), lambda i: (i, 0)), pl.BlockSpec((1, d), lambda i: (0, 0))],
        out_specs=pl.BlockSpec((tm, d), lambda i: (i, 0)),
        out_shape=jax.ShapeDtypeStruct((t, d), F32),
        compiler_params=_params(("parallel",), 32),
        name="final_norm",
    )(x, gain)


def kernel(x, ffn_norm, ffn_w_in, ffn_w_out, mix_norm, w_in, b_gate, hgrn_lb, hgrn_norm,
           w_proj_attn, w_proj_hgrn, w_out, final_norm):
    bsz, seq, d = x.shape
    assert seq == SEQ and d == D_MODEL
    t = bsz * seq

    w_gu = jnp.stack([ffn_w_in[..., :D_FF], ffn_w_in[..., D_FF:]], axis=2).astype(BF16)
    w_gu_tail = jnp.concatenate([ffn_w_in[..., D_FF - FFN_TAIL:D_FF], ffn_w_in[..., -FFN_TAIL:]],
                                axis=-1).astype(BF16)
    w_down = ffn_w_out.astype(BF16)
    n_attn = 3 * ATTN_WIDTH
    w_mix = jnp.concatenate([w_in[..., n_attn:], w_in[..., :n_attn]], axis=-1).astype(BF16)
    w_pa = w_proj_attn.astype(BF16)
    w_pm = w_proj_hgrn.astype(BF16)
    w_o = w_out.astype(BF16)
    ffn_gain = ffn_norm.reshape(DEPTH, 2, 1, d)
    mix_gain = mix_norm.reshape(DEPTH, 1, d)
    gate_bias = b_gate.reshape(DEPTH, 1, 2 * d)
    masks = jnp.asarray(_hgrn_masks())

    xf = x.reshape(t, d)
    for l in range(DEPTH):
        xf = _ffn(xf, ffn_gain, w_gu, w_gu_tail, w_down, l, 0)
        z = _norm_matmul(xf, mix_gain, w_mix, l)
        z3 = z.reshape(bsz, seq, Z_WIDTH)
        attn = _attention(z3)
        hgrn = _hgrn(z3, hgrn_lb, hgrn_norm, masks, l)
        xf = _combine(xf, attn.reshape(t, GROUP_WIDTH), hgrn.reshape(t, HGRN_WIDTH), z,
                      gate_bias, w_pa, w_pm, w_o, l)
        xf = _ffn(xf, ffn_gain, w_gu, w_gu_tail, w_down, l, 1)
    return _final_norm(xf, final_norm.reshape(1, d)).reshape(bsz, seq, d)
```

```python
import functools

import numpy as np
import jax
import jax.numpy as jnp
from jax import lax
from jax.experimental import pallas as pl
from jax.experimental.pallas import tpu as pltpu

F32 = jnp.float32
BF16 = jnp.bfloat16

D_MODEL = 2048
SEQ = 2048
DEPTH = 4
HEAD_DIM = 128
ATTN_HEADS = 4
DILATED_GROUPS = ((128, 1), (512, 4), (2048, 16))
GROUP_WIDTH = ATTN_HEADS * HEAD_DIM
ATTN_WIDTH = len(DILATED_GROUPS) * GROUP_WIDTH
HGRN_HEADS = 8
HGRN_DIM = 128
HGRN_WIDTH = HGRN_HEADS * HGRN_DIM
Z_GATE = 4 * HGRN_WIDTH
Z_ATTN = Z_GATE + 2 * D_MODEL
Z_WIDTH = Z_ATTN + 3 * ATTN_WIDTH
D_FF = 5504
FFN_TILE = 768
FFN_TAIL = 128
MACARON_WEIGHT = 0.5
EPS = 1e-6
MASK_VALUE = -1e30
LANES = 128
ATTN_BLOCK = 128
HGRN_CHUNK = 128
HGRN_LEVELS = (1, 2, 4, 8, 16, 32, 64)
MIB = 1024 * 1024


def _rms_normalize(x, gain):
    ms = jnp.mean(x * x, axis=-1, keepdims=True)
    return x * lax.rsqrt(ms + EPS) * gain


def _dot(a, b):
    return jnp.dot(a, b, preferred_element_type=F32)


def _dot_nt(a, b):
    return lax.dot_general(a, b, (((1,), (1,)), ((), ())), preferred_element_type=F32)


def _dot_tn(a, b):
    return lax.dot_general(a, b, (((0,), (0,)), ((), ())), preferred_element_type=F32)


def _params(semantics, vmem_mib):
    return pltpu.CompilerParams(dimension_semantics=semantics, vmem_limit_bytes=vmem_mib * MIB)


def _split_ffn_in_kernel(src_ref, dst_ref, tail_ref):
    v = src_ref[...].astype(BF16)
    dst_ref[...] = v
    tail_ref[...] = v[:, D_FF - FFN_TAIL:]


def _split_ffn_in(ffn_w_in, *, tr=256):
    depth, two, d, _ = ffn_w_in.shape
    return pl.pallas_call(
        _split_ffn_in_kernel,
        grid=(depth, two, 2, d // tr),
        in_specs=[pl.BlockSpec((None, None, tr, D_FF), lambda l, w, k, r: (l, w, r, k))],
        out_specs=[pl.BlockSpec((None, None, None, tr, D_FF), lambda l, w, k, r: (l, w, k, r, 0)),
                   pl.BlockSpec((None, None, tr, FFN_TAIL), lambda l, w, k, r: (l, w, r, k))],
        out_shape=[jax.ShapeDtypeStruct((depth, two, 2, d, D_FF), BF16),
                   jax.ShapeDtypeStruct((depth, two, d, 2 * FFN_TAIL), BF16)],
        compiler_params=_params(("parallel",) * 4, 32),
        name="split_ffn_in",
    )(ffn_w_in)


def _cast_kernel(src_ref, dst_ref):
    dst_ref[...] = src_ref[...].astype(BF16)


def _reorder_mix_in(w_in, *, tn=512):
    depth, d, n = w_in.shape
    nblk = n // tn
    shift = Z_ATTN // tn
    return pl.pallas_call(
        _cast_kernel,
        grid=(depth, nblk),
        in_specs=[pl.BlockSpec((None, d, tn), lambda l, j: (l, 0, j))],
        out_specs=pl.BlockSpec((None, d, tn), lambda l, j: (l, 0, lax.rem(j + shift, nblk))),
        out_shape=jax.ShapeDtypeStruct((depth, d, n), BF16),
        compiler_params=_params(("parallel", "parallel"), 32),
        name="reorder_mix_in",
    )(w_in)


def _swiglu_act(g, u):
    return (MACARON_WEIGHT * g * jax.nn.sigmoid(g)) * u


def _ffn_kernel(x_ref, gain_ref, wg_ref, wu_ref, wo_ref, wgu_tail_ref, wo_tail_ref, o_ref, h_ref):
    @pl.when(pl.program_id(1) == 0)
    def _():
        x = x_ref[...]
        h_ref[...] = _rms_normalize(x, gain_ref[...]).astype(BF16)
        gu = _dot(h_ref[...], wgu_tail_ref[...])
        a = _swiglu_act(gu[:, :FFN_TAIL], gu[:, FFN_TAIL:])
        o_ref[...] = x + _dot(a.astype(BF16), wo_tail_ref[...])

    h = h_ref[...]
    a = _swiglu_act(_dot(h, wg_ref[...]), _dot(h, wu_ref[...]))
    o_ref[...] += _dot(a.astype(BF16), wo_ref[...])


def _ffn(x, gain, w_gu, w_gu_tail, w_down, layer, which, *, tm=512):
    t, d = x.shape
    tf = FFN_TILE
    tail_block = (D_FF - FFN_TAIL) // FFN_TAIL
    return pl.pallas_call(
        _ffn_kernel,
        grid=(t // tm, (D_FF - FFN_TAIL) // tf),
        in_specs=[
            pl.BlockSpec((tm, d), lambda i, j: (i, 0)),
            pl.BlockSpec((None, None, 1, d), lambda i, j: (layer, which, 0, 0)),
            pl.BlockSpec((None, None, None, d, tf), lambda i, j: (layer, which, 0, 0, j)),
            pl.BlockSpec((None, None, None, d, tf), lambda i, j: (layer, which, 1, 0, j)),
            pl.BlockSpec((None, None, tf, d), lambda i, j: (layer, which, j, 0)),
            pl.BlockSpec((None, None, d, 2 * FFN_TAIL), lambda i, j: (layer, which, 0, 0)),
            pl.BlockSpec((None, None, FFN_TAIL, d), lambda i, j: (layer, which, tail_block, 0)),
        ],
        out_specs=pl.BlockSpec((tm, d), lambda i, j: (i, 0)),
        out_shape=jax.ShapeDtypeStruct((t, d), F32),
        scratch_shapes=[pltpu.VMEM((tm, d), BF16)],
        compiler_params=_params(("parallel", "arbitrary"), 52),
        name="ffn",
    )(x, gain, w_gu, w_gu, w_down, w_gu_tail, w_down)


def _norm_matmul_kernel(x_ref, gain_ref, w_ref, o_ref, h_ref):
    @pl.when(pl.program_id(1) == 0)
    def _():
        h_ref[...] = _rms_normalize(x_ref[...], gain_ref[...]).astype(BF16)

    o_ref[...] = _dot(h_ref[...], w_ref[...]).astype(o_ref.dtype)


def _norm_matmul(x, gain, w, layer, *, tm=1024, tn=1280):
    t, d = x.shape
    n_cols = w.shape[-1]
    return pl.pallas_call(
        _norm_matmul_kernel,
        grid=(t // tm, n_cols // tn),
        in_specs=[
            pl.BlockSpec((tm, d), lambda i, j: (i, 0)),
            pl.BlockSpec((None, 1, d), lambda i, j: (layer, 0, 0)),
            pl.BlockSpec((None, d, tn), lambda i, j: (layer, 0, j)),
        ],
        out_specs=pl.BlockSpec((tm, tn), lambda i, j: (i, j)),
        out_shape=jax.ShapeDtypeStruct((t, n_cols), BF16),
        scratch_shapes=[pltpu.VMEM((tm, d), BF16)],
        compiler_params=_params(("parallel", "arbitrary"), 48),
        name="in_proj",
    )(x, gain, w)


def _attn_kernel(q0, k0, v0, q1, k1, v1, q2, k2, v2, o_ref,
                 qs, ks, vs, og0, og1, og2, lg0, lg1, lg2):
    nb = ATTN_BLOCK
    scale = HEAD_DIM ** -0.5
    row2 = lax.broadcasted_iota(jnp.int32, (nb, 2 * nb), 0)
    col2 = lax.broadcasted_iota(jnp.int32, (nb, 2 * nb), 1)
    band = (col2 >= row2) & (col2 <= row2 + nb)
    row1 = lax.broadcasted_iota(jnp.int32, (nb, nb), 0)
    col1 = lax.broadcasted_iota(jnp.int32, (nb, nb), 1)
    causal = col1 <= row1

    def block(q, k, v, mask):
        s = _dot_nt(q, k) * scale
        s = jnp.where(mask, s, MASK_VALUE)
        m = jnp.max(s, axis=-1, keepdims=True)
        p = jnp.exp(s - m)
        l = jnp.sum(p, axis=-1, keepdims=True)
        o = _dot(p.astype(BF16), v)
        return o / l, m + jnp.log(l)

    def run_group(dilation, q_ref, k_ref, v_ref, og, lg):
        r = dilation
        nblk = SEQ // (r * nb)
        if r > 1:
            qs[...] = q_ref[...].astype(F32)
            ks[...] = k_ref[...].astype(F32)
            vs[...] = v_ref[...].astype(F32)
        for c in range(r):
            for n in range(nblk):
                first = n == 0
                nk = nb if first else 2 * nb
                q_start = c + r * nb * n
                k_start = q_start if first else q_start - r * nb
                if r == 1:
                    q = q_ref[pl.ds(q_start, nb), :]
                    k = k_ref[pl.ds(k_start, nk), :]
                    v = v_ref[pl.ds(k_start, nk), :]
                    rows = pl.ds(q_start, nb)
                else:
                    q = qs[pl.ds(q_start, nb, stride=r), :].astype(BF16)
                    k = ks[pl.ds(k_start, nk, stride=r), :].astype(BF16)
                    v = vs[pl.ds(k_start, nk, stride=r), :].astype(BF16)
                    rows = pl.ds(q_start, nb, stride=r)
                o, lse = block(q, k, v, causal if first else band)
                og[rows, :] = o
                lg[rows, :] = jnp.broadcast_to(lse, (nb, LANES))

    run_group(DILATED_GROUPS[0][1], q0, k0, v0, og0, lg0)
    run_group(DILATED_GROUPS[1][1], q1, k1, v1, og1, lg1)
    run_group(DILATED_GROUPS[2][1], q2, k2, v2, og2, lg2)

    tile = 256
    for t in range(SEQ // tile):
        rows = pl.ds(t * tile, tile)
        l0, l1, l2 = lg0[rows, :], lg1[rows, :], lg2[rows, :]
        mx = jnp.maximum(jnp.maximum(l0, l1), l2)
        w0, w1, w2 = jnp.exp(l0 - mx), jnp.exp(l1 - mx), jnp.exp(l2 - mx)
        num = w0 * og0[rows, :] + w1 * og1[rows, :] + w2 * og2[rows, :]
        o_ref[rows, :] = (num / (w0 + w1 + w2)).astype(BF16)


def _attention(z):
    b = z.shape[0]
    heads_per_part = ATTN_WIDTH // HEAD_DIM

    def spec(part, group):
        base = Z_ATTN // HEAD_DIM + part * heads_per_part + group * ATTN_HEADS
        return pl.BlockSpec((None, SEQ, HEAD_DIM), lambda bi, h: (bi, 0, base + h))

    in_specs = [spec(part, group) for group in range(3) for part in range(3)]
    seq_f32 = pltpu.VMEM((SEQ, HEAD_DIM), F32)
    return pl.pallas_call(
        _attn_kernel,
        grid=(b, ATTN_HEADS),
        in_specs=in_specs,
        out_specs=pl.BlockSpec((None, SEQ, HEAD_DIM), lambda bi, h: (bi, 0, h)),
        out_shape=jax.ShapeDtypeStruct((b, SEQ, GROUP_WIDTH), BF16),
        scratch_shapes=[seq_f32] * 9,
        compiler_params=_params(("parallel", "parallel"), 40),
        name="dilated_attn",
    )(*([z] * 9))


def _hgrn_masks():
    c = HGRN_CHUNK
    t = np.arange(c)[:, None]
    s = np.arange(c)[None, :]
    masks = []
    for h in HGRN_LEVELS:
        masks.append((t // (2 * h) == s // (2 * h)) & (t % (2 * h) >= h) & (s % (2 * h) < h))
    masks.append(t == s)
    return np.stack(masks).astype(np.float32)


def _hgrn_kernel(layer, zq, zf, zi, zg, lb_ref, gn_ref, mk_ref, o_ref, st_ref, b_sc):
    c = HGRN_CHUNK
    tc = zq.shape[0]

    @pl.when(pl.program_id(1) == 0)
    def _():
        st_ref[...] = jnp.zeros_like(st_ref)

    lb_all = lb_ref[...]
    e = jnp.exp(lb_all - jnp.max(lb_all, axis=0, keepdims=True))
    p = e / jnp.sum(e, axis=0, keepdims=True)
    lower = jnp.zeros((1, HGRN_WIDTH), F32)
    for i in range(1, layer + 1):
        lower = lower + p[i:i + 1, :]
    gain = gn_ref[layer:layer + 1, :]

    rowi = lax.broadcasted_iota(jnp.int32, (c, HGRN_DIM), 0)
    sub8 = rowi % 8
    odd2 = rowi % 2 == 1
    r4 = rowi % 4
    sign_bit = jnp.uint32(0x80000000)
    head_cols = [slice(hd * HGRN_DIM, (hd + 1) * HGRN_DIM) for hd in range(HGRN_HEADS)]
    head_lb = [lower[:, cols] for cols in head_cols]
    head_oml = [1.0 - lb for lb in head_lb]

    def roll_in_vreg(v, d):
        return jnp.concatenate([pltpu.roll(v[r:r + 8], d, axis=0) for r in range(0, c, 8)], axis=0)

    def chunk(ci, carry):
        r0 = pl.multiple_of(ci * c, c)
        rows = pl.ds(r0, c)
        for hd in range(HGRN_HEADS):
            cols = head_cols[hd]
            lb, oml = head_lb[hd], head_oml[hd]
            q = zq[rows, cols].astype(F32)
            fl = zf[rows, cols].astype(F32)
            iv = zi[rows, cols]
            ef = jnp.exp(-fl)
            sg = 1.0 / (1.0 + ef)
            lf = jnp.log(lb + oml * sg)
            kk = oml * (ef * sg)
            qf = q / (1.0 + jnp.exp(-q))

            b = lf
            for d in (1, 2, 4):
                b = b + jnp.where(sub8 >= d, roll_in_vreg(b, d), 0.0)
            groups = [b[0:8]]
            for r in range(8, c, 8):
                groups.append(b[r:r + 8] + jnp.broadcast_to(groups[-1][7:8], (8, HGRN_DIM)))
            b = jnp.concatenate(groups, axis=0)
            b_sc[hd] = b
            b_last = b_sc[hd, c - 1:c, :]

            qb = qf.astype(BF16)
            kb = kk.astype(BF16)
            scores = mk_ref[len(HGRN_LEVELS)] * jnp.sum(qf * kk, axis=-1, keepdims=True)
            for lev, h in enumerate(HGRN_LEVELS):
                if h == 1:
                    g = jnp.where(odd2, roll_in_vreg(b, 1), b)
                elif h == 2:
                    g = jnp.where(r4 == 0, roll_in_vreg(b, 7),
                                  jnp.where(r4 == 1, b,
                                            jnp.where(r4 == 2, roll_in_vreg(b, 1), roll_in_vreg(b, 2))))
                else:
                    g = jnp.concatenate(
                        [jnp.broadcast_to(b_sc[hd, pl.ds(p0 + h - 1, 1), :], (2 * h, HGRN_DIM))
                         for p0 in range(0, c, 2 * h)], axis=0)
                neg_abs = lax.bitcast_convert_type(
                    lax.bitcast_convert_type(b - g, jnp.uint32) | sign_bit, F32)
                ed = jnp.exp(neg_abs).astype(BF16)
                scores = scores + _dot_nt(qb * ed, kb * ed) * mk_ref[lev]

            st = st_ref[hd]
            o = _dot_nt((qf * jnp.exp(b)).astype(BF16), st.astype(BF16))
            o = o + _dot(scores.astype(BF16), iv)
            kd = (kk * jnp.exp(b_last - b)).astype(BF16)
            st_ref[hd] = jnp.exp(b_last) * st + _dot_tn(iv, kd)

            gt = zg[rows, cols].astype(F32)
            y = _rms_normalize(o, gain) * (gt / (1.0 + jnp.exp(-gt)))
            o_ref[rows, cols] = y.astype(BF16)
        return carry

    lax.fori_loop(0, tc // c, chunk, 0)


def _hgrn(z, hgrn_lb, hgrn_norm, masks, layer, *, tc=512):
    b = z.shape[0]

    def spec(part):
        return pl.BlockSpec((None, tc, HGRN_WIDTH), lambda bi, t: (bi, t, part))

    return pl.pallas_call(
        functools.partial(_hgrn_kernel, layer),
        grid=(b, SEQ // tc),
        in_specs=[spec(0), spec(1), spec(2), spec(3),
                  pl.BlockSpec(hgrn_lb.shape, lambda bi, t: (0, 0)),
                  pl.BlockSpec(hgrn_norm.shape, lambda bi, t: (0, 0)),
                  pl.BlockSpec(masks.shape, lambda bi, t: (0, 0, 0))],
        out_specs=pl.BlockSpec((None, tc, HGRN_WIDTH), lambda bi, t: (bi, t, 0)),
        out_shape=jax.ShapeDtypeStruct((b, SEQ, HGRN_WIDTH), BF16),
        scratch_shapes=[pltpu.VMEM((HGRN_HEADS, HGRN_DIM, HGRN_DIM), F32),
                        pltpu.VMEM((HGRN_HEADS, HGRN_CHUNK, HGRN_DIM), F32)],
        compiler_params=_params(("parallel", "arbitrary"), 40),
        name="hgrn2",
    )(z, z, z, z, hgrn_lb, hgrn_norm, masks)


def _combine_kernel(x_ref, at_ref, hg_ref, gl_ref, bg_ref, wpa_ref, wpm_ref, wo_ref, o_ref, y_ref):
    d = o_ref.shape[1]
    tn = 512
    at = at_ref[...]
    hg = hg_ref[...]
    for c0 in range(0, d, tn):
        ga = jax.nn.sigmoid(gl_ref[:, c0:c0 + tn].astype(F32) + bg_ref[:, c0:c0 + tn])
        gm = jax.nn.sigmoid(gl_ref[:, d + c0:d + c0 + tn].astype(F32) + bg_ref[:, d + c0:d + c0 + tn])
        y = ga * _dot(at, wpa_ref[:, c0:c0 + tn]) + gm * _dot(hg, wpm_ref[:, c0:c0 + tn])
        y_ref[:, c0:c0 + tn] = y.astype(BF16)
    o_ref[...] = x_ref[...] + _dot(y_ref[...], wo_ref[...])


def _combine(x, attn, hgrn, z, b_gate, w_pa, w_pm, w_o, layer, *, tm=256):
    t, d = x.shape

    def rows(width, col_block=0):
        return pl.BlockSpec((tm, width), lambda i: (i, col_block))

    def whole(a):
        return pl.BlockSpec((None,) + a.shape[1:], lambda i: (layer,) + (0,) * (a.ndim - 1))

    return pl.pallas_call(
        _combine_kernel,
        grid=(t // tm,),
        in_specs=[rows(d), rows(GROUP_WIDTH), rows(HGRN_WIDTH), rows(2 * d, Z_GATE // (2 * d)),
                  whole(b_gate), whole(w_pa), whole(w_pm), whole(w_o)],
        out_specs=rows(d),
        out_shape=jax.ShapeDtypeStruct((t, d), F32),
        scratch_shapes=[pltpu.VMEM((tm, d), BF16)],
        compiler_params=_params(("parallel",), 56),
        name="merge_out_proj",
    )(x, attn, hgrn, z, b_gate, w_pa, w_pm, w_o)


def _final_norm_kernel(x_ref, gain_ref, o_ref):
    o_ref[...] = _rms_normalize(x_ref[...], gain_ref[...])


def _final_norm(x, gain, *, tm=512):
    t, d = x.shape
    return pl.pallas_call(
        _final_norm_kernel,
        grid=(t // tm,),
        in_specs=[pl.BlockSpec((tm, d), lambda i: (i, 0)), pl.BlockSpec((1, d), lambda i: (0, 0))],
        out_specs=pl.BlockSpec((tm, d), lambda i: (i, 0)),
        out_shape=jax.ShapeDtypeStruct((t, d), F32),
        compiler_params=_params(("parallel",), 32),
        name="final_norm",
    )(x, gain)


def kernel(x, ffn_norm, ffn_w_in, ffn_w_out, mix_norm, w_in, b_gate, hgrn_lb, hgrn_norm,
           w_proj_attn, w_proj_hgrn, w_out, final_norm):
    bsz, seq, d = x.shape
    assert seq == SEQ and d == D_MODEL
    t = bsz * seq

    w_gu, w_gu_tail = _split_ffn_in(ffn_w_in)
    w_down = ffn_w_out.astype(BF16)
    w_mix = _reorder_mix_in(w_in)
    w_pa = w_proj_attn.astype(BF16)
    w_pm = w_proj_hgrn.astype(BF16)
    w_o = w_out.astype(BF16)
    ffn_gain = ffn_norm.reshape(DEPTH, 2, 1, d)
    mix_gain = mix_norm.reshape(DEPTH, 1, d)
    gate_bias = b_gate.reshape(DEPTH, 1, 2 * d)
    masks = jnp.asarray(_hgrn_masks())

    xf = x.reshape(t, d)
    for l in range(DEPTH):
        xf = _ffn(xf, ffn_gain, w_gu, w_gu_tail, w_down, l, 0)
        z = _norm_matmul(xf, mix_gain, w_mix, l)
        z3 = z.reshape(bsz, seq, Z_WIDTH)
        attn = _attention(z3)
        hgrn = _hgrn(z3, hgrn_lb, hgrn_norm, masks, l)
        xf = _combine(xf, attn.reshape(t, GROUP_WIDTH), hgrn.reshape(t, HGRN_WIDTH), z,
                      gate_bias, w_pa, w_pm, w_o, l)
        xf = _ffn(xf, ffn_gain, w_gu, w_gu_tail, w_down, l, 1)
    return _final_norm(xf, final_norm.reshape(1, d)).reshape(bsz, seq, d)
```

```python
import functools

import numpy as np
import jax
import jax.numpy as jnp
from jax import lax
from jax.experimental import pallas as pl
from jax.experimental.pallas import tpu as pltpu

F32 = jnp.float32
BF16 = jnp.bfloat16

D_MODEL = 2048
SEQ = 2048
DEPTH = 4
HEAD_DIM = 128
ATTN_HEADS = 4
DILATED_GROUPS = ((128, 1), (512, 4), (2048, 16))
GROUP_WIDTH = ATTN_HEADS * HEAD_DIM
ATTN_WIDTH = len(DILATED_GROUPS) * GROUP_WIDTH
HGRN_HEADS = 8
HGRN_DIM = 128
HGRN_WIDTH = HGRN_HEADS * HGRN_DIM
Z_GATE = 4 * HGRN_WIDTH
Z_ATTN = Z_GATE + 2 * D_MODEL
Z_WIDTH = Z_ATTN + 3 * ATTN_WIDTH
D_FF = 5504
FFN_TILE = 768
FFN_TAIL = 128
FFN_BAND = 80
MACARON_WEIGHT = 0.5
EPS = 1e-6
MASK_VALUE = -1e30
LANES = 128
ATTN_BLOCK = 128
HGRN_CHUNK = 128
HGRN_LEVELS = (1, 2, 4, 8, 16, 32, 64)
MIB = 1024 * 1024


def _rms_normalize(x, gain):
    ms = jnp.mean(x * x, axis=-1, keepdims=True)
    return x * lax.rsqrt(ms + EPS) * gain


def _dot(a, b):
    return jnp.dot(a, b, preferred_element_type=F32)


def _dot_nt(a, b):
    return lax.dot_general(a, b, (((1,), (1,)), ((), ())), preferred_element_type=F32)


def _dot_tn(a, b):
    return lax.dot_general(a, b, (((0,), (0,)), ((), ())), preferred_element_type=F32)


def _params(semantics, vmem_mib):
    return pltpu.CompilerParams(dimension_semantics=semantics, vmem_limit_bytes=vmem_mib * MIB)


def _split_ffn_in_kernel(src_ref, dst_ref, tail_ref):
    v = src_ref[...].astype(BF16)
    dst_ref[...] = v
    tail_ref[...] = v[:, D_FF - FFN_TAIL:]


def _split_ffn_in(ffn_w_in, *, tr=256):
    depth, two, d, _ = ffn_w_in.shape
    return pl.pallas_call(
        _split_ffn_in_kernel,
        grid=(depth, two, 2, d // tr),
        in_specs=[pl.BlockSpec((None, None, tr, D_FF), lambda l, w, k, r: (l, w, r, k))],
        out_specs=[pl.BlockSpec((None, None, None, tr, D_FF), lambda l, w, k, r: (l, w, k, r, 0)),
                   pl.BlockSpec((None, None, tr, FFN_TAIL), lambda l, w, k, r: (l, w, r, k))],
        out_shape=[jax.ShapeDtypeStruct((depth, two, 2, d, D_FF), BF16),
                   jax.ShapeDtypeStruct((depth, two, d, 2 * FFN_TAIL), BF16)],
        compiler_params=_params(("parallel",) * 4, 32),
        name="split_ffn_in",
    )(ffn_w_in)


def _cast_kernel(src_ref, dst_ref):
    dst_ref[...] = src_ref[...].astype(BF16)


def _reorder_mix_in(w_in, *, tn=512):
    depth, d, n = w_in.shape
    nblk = n // tn
    shift = Z_ATTN // tn
    return pl.pallas_call(
        _cast_kernel,
        grid=(depth, nblk),
        in_specs=[pl.BlockSpec((None, d, tn), lambda l, j: (l, 0, j))],
        out_specs=pl.BlockSpec((None, d, tn), lambda l, j: (l, 0, lax.rem(j + shift, nblk))),
        out_shape=jax.ShapeDtypeStruct((depth, d, n), BF16),
        compiler_params=_params(("parallel", "parallel"), 32),
        name="reorder_mix_in",
    )(w_in)


def _swiglu_act(g, u):
    return (MACARON_WEIGHT * g * jax.nn.sigmoid(g)) * u


def _ffn_kernel(norm_output, x_ref, xn_ref, gain_ref, wg_ref, wu_ref, wo_ref, wgu_tail_ref,
                wo_tail_ref, out_gain_ref, o_ref, h_ref):
    i = pl.program_id(0)
    j = pl.program_id(1)
    tm = x_ref.shape[0]
    slot = lax.rem(i, 2)
    gain = gain_ref[...]

    @pl.when((i == 0) & (j == 0))
    def _():
        h_ref[0] = _rms_normalize(x_ref[...], gain).astype(BF16)

    @pl.when(j == 0)
    def _():
        gu = jnp.concatenate([_dot(h_ref[slot, r:r + tm // 2, :], wgu_tail_ref[...])
                              for r in (0, tm // 2)], axis=0)
        a = _swiglu_act(gu[:, :FFN_TAIL], gu[:, FFN_TAIL:]).astype(BF16)
        half = o_ref.shape[1] // 2
        for c0 in (0, half):
            o_ref[:, c0:c0 + half] = x_ref[:, c0:c0 + half] + _dot(a, wo_tail_ref[:, c0:c0 + half])

    h = h_ref[slot]
    a = _swiglu_act(_dot(h, wg_ref[...]), _dot(h, wu_ref[...]))
    o_ref[...] += _dot(a.astype(BF16), wo_ref[...])

    r0 = pl.multiple_of(jnp.minimum(j * FFN_BAND, tm - FFN_BAND), 16)
    rows = pl.ds(r0, FFN_BAND)
    h_ref[1 - slot, rows, :] = _rms_normalize(xn_ref[rows, :], gain).astype(BF16)

    if norm_output:
        @pl.when(j == pl.num_programs(1) - 1)
        def _():
            o_ref[...] = _rms_normalize(o_ref[...], out_gain_ref[...])


def _ffn(x, gain, w_gu, w_gu_tail, w_down, out_gain, layer, which, *, norm_output=False, tm=512):
    t, d = x.shape
    tf = FFN_TILE
    n_row_tiles = t // tm
    n_col_tiles = (D_FF - FFN_TAIL) // tf
    assert n_col_tiles * FFN_BAND >= tm and tm % 16 == 0 and FFN_BAND % 16 == 0
    tail_block = (D_FF - FFN_TAIL) // FFN_TAIL
    return pl.pallas_call(
        functools.partial(_ffn_kernel, norm_output),
        grid=(n_row_tiles, n_col_tiles),
        in_specs=[
            pl.BlockSpec((tm, d), lambda i, j: (i, 0)),
            pl.BlockSpec((tm, d), lambda i, j: (jnp.minimum(i + 1, n_row_tiles - 1), 0)),
            pl.BlockSpec((None, None, 1, d), lambda i, j: (layer, which, 0, 0)),
            pl.BlockSpec((None, None, None, d, tf), lambda i, j: (layer, which, 0, 0, j)),
            pl.BlockSpec((None, None, None, d, tf), lambda i, j: (layer, which, 1, 0, j)),
            pl.BlockSpec((None, None, tf, d), lambda i, j: (layer, which, j, 0)),
            pl.BlockSpec((None, None, d, 2 * FFN_TAIL), lambda i, j: (layer, which, 0, 0)),
            pl.BlockSpec((None, None, FFN_TAIL, d), lambda i, j: (layer, which, tail_block, 0)),
            pl.BlockSpec((1, d), lambda i, j: (0, 0)),
        ],
        out_specs=pl.BlockSpec((tm, d), lambda i, j: (i, 0)),
        out_shape=jax.ShapeDtypeStruct((t, d), F32),
        scratch_shapes=[pltpu.VMEM((2, tm, d), BF16)],
        compiler_params=_params(("arbitrary", "arbitrary"), 58),
        name="ffn",
    )(x, x, gain, w_gu, w_gu, w_down, w_gu_tail, w_down, out_gain)


def _norm_matmul_kernel(x_ref, gain_ref, w_ref, o_ref, h_ref):
    @pl.when(pl.program_id(1) == 0)
    def _():
        h_ref[...] = _rms_normalize(x_ref[...], gain_ref[...]).astype(BF16)

    o_ref[...] = _dot(h_ref[...], w_ref[...]).astype(o_ref.dtype)


def _norm_matmul(x, gain, w, layer, *, tm=1024, tn=1280):
    t, d = x.shape
    n_cols = w.shape[-1]
    return pl.pallas_call(
        _norm_matmul_kernel,
        grid=(t // tm, n_cols // tn),
        in_specs=[
            pl.BlockSpec((tm, d), lambda i, j: (i, 0)),
            pl.BlockSpec((None, 1, d), lambda i, j: (layer, 0, 0)),
            pl.BlockSpec((None, d, tn), lambda i, j: (layer, 0, j)),
        ],
        out_specs=pl.BlockSpec((tm, tn), lambda i, j: (i, j)),
        out_shape=jax.ShapeDtypeStruct((t, n_cols), BF16),
        scratch_shapes=[pltpu.VMEM((tm, d), BF16)],
        compiler_params=_params(("parallel", "arbitrary"), 48),
        name="in_proj",
    )(x, gain, w)


def _attn_kernel(q0, k0, v0, q1, k1, v1, q2, k2, v2, o_ref,
                 qs, ks, vs, og0, og1, og2, lg0, lg1, lg2):
    nb = ATTN_BLOCK
    scale = HEAD_DIM ** -0.5
    row2 = lax.broadcasted_iota(jnp.int32, (nb, 2 * nb), 0)
    col2 = lax.broadcasted_iota(jnp.int32, (nb, 2 * nb), 1)
    band = (col2 >= row2) & (col2 <= row2 + nb)
    row1 = lax.broadcasted_iota(jnp.int32, (nb, nb), 0)
    col1 = lax.broadcasted_iota(jnp.int32, (nb, nb), 1)
    causal = col1 <= row1

    def block(q, k, v, mask):
        s = _dot_nt(q, k) * scale
        s = jnp.where(mask, s, MASK_VALUE)
        m = jnp.max(s, axis=-1, keepdims=True)
        p = jnp.exp(s - m)
        l = jnp.sum(p, axis=-1, keepdims=True)
        o = _dot(p.astype(BF16), v)
        return o / l, m + jnp.log(l)

    def run_group(dilation, q_ref, k_ref, v_ref, og, lg):
        r = dilation
        nblk = SEQ // (r * nb)
        if r > 1:
            qs[...] = q_ref[...].astype(F32)
            ks[...] = k_ref[...].astype(F32)
            vs[...] = v_ref[...].astype(F32)
        for c in range(r):
            for n in range(nblk):
                first = n == 0
                nk = nb if first else 2 * nb
                q_start = c + r * nb * n
                k_start = q_start if first else q_start - r * nb
                if r == 1:
                    q = q_ref[pl.ds(q_start, nb), :]
                    k = k_ref[pl.ds(k_start, nk), :]
                    v = v_ref[pl.ds(k_start, nk), :]
                    rows = pl.ds(q_start, nb)
                else:
                    q = qs[pl.ds(q_start, nb, stride=r), :].astype(BF16)
                    k = ks[pl.ds(k_start, nk, stride=r), :].astype(BF16)
                    v = vs[pl.ds(k_start, nk, stride=r), :].astype(BF16)
                    rows = pl.ds(q_start, nb, stride=r)
                o, lse = block(q, k, v, causal if first else band)
                og[rows, :] = o
                lg[rows, :] = jnp.broadcast_to(lse, (nb, LANES))

    run_group(DILATED_GROUPS[0][1], q0, k0, v0, og0, lg0)
    run_group(DILATED_GROUPS[1][1], q1, k1, v1, og1, lg1)
    run_group(DILATED_GROUPS[2][1], q2, k2, v2, og2, lg2)

    tile = 256
    for t in range(SEQ // tile):
        rows = pl.ds(t * tile, tile)
        l0, l1, l2 = lg0[rows, :], lg1[rows, :], lg2[rows, :]
        mx = jnp.maximum(jnp.maximum(l0, l1), l2)
        w0, w1, w2 = jnp.exp(l0 - mx), jnp.exp(l1 - mx), jnp.exp(l2 - mx)
        num = w0 * og0[rows, :] + w1 * og1[rows, :] + w2 * og2[rows, :]
        o_ref[rows, :] = (num / (w0 + w1 + w2)).astype(BF16)


def _attention(z):
    b = z.shape[0]
    heads_per_part = ATTN_WIDTH // HEAD_DIM

    def spec(part, group):
        base = Z_ATTN // HEAD_DIM + part * heads_per_part + group * ATTN_HEADS
        return pl.BlockSpec((None, SEQ, HEAD_DIM), lambda bi, h: (bi, 0, base + h))

    in_specs = [spec(part, group) for group in range(3) for part in range(3)]
    seq_f32 = pltpu.VMEM((SEQ, HEAD_DIM), F32)
    return pl.pallas_call(
        _attn_kernel,
        grid=(b, ATTN_HEADS),
        in_specs=in_specs,
        out_specs=pl.BlockSpec((None, SEQ, HEAD_DIM), lambda bi, h: (bi, 0, h)),
        out_shape=jax.ShapeDtypeStruct((b, SEQ, GROUP_WIDTH), BF16),
        scratch_shapes=[seq_f32] * 9,
        compiler_params=_params(("parallel", "parallel"), 40),
        name="dilated_attn",
    )(*([z] * 9))


def _hgrn_masks():
    c = HGRN_CHUNK
    t = np.arange(c)[:, None]
    s = np.arange(c)[None, :]
    masks = []
    for h in HGRN_LEVELS:
        masks.append((t // (2 * h) == s // (2 * h)) & (t % (2 * h) >= h) & (s % (2 * h) < h))
    masks.append(t == s)
    return np.stack(masks).astype(np.float32)


def _hgrn_kernel(layer, zq, zf, zi, zg, lb_ref, gn_ref, mk_ref, o_ref, st_ref, b_sc):
    c = HGRN_CHUNK
    tc = zq.shape[0]

    @pl.when(pl.program_id(1) == 0)
    def _():
        st_ref[...] = jnp.zeros_like(st_ref)

    lb_all = lb_ref[...]
    e = jnp.exp(lb_all - jnp.max(lb_all, axis=0, keepdims=True))
    p = e / jnp.sum(e, axis=0, keepdims=True)
    lower = jnp.zeros((1, HGRN_WIDTH), F32)
    for i in range(1, layer + 1):
        lower = lower + p[i:i + 1, :]
    gain = gn_ref[layer:layer + 1, :]

    rowi = lax.broadcasted_iota(jnp.int32, (c, HGRN_DIM), 0)
    sub8 = rowi % 8
    odd2 = rowi % 2 == 1
    r4 = rowi % 4
    sign_bit = jnp.uint32(0x80000000)
    head_cols = [slice(hd * HGRN_DIM, (hd + 1) * HGRN_DIM) for hd in range(HGRN_HEADS)]
    head_lb = [lower[:, cols] for cols in head_cols]
    head_oml = [1.0 - lb for lb in head_lb]

    def roll_in_vreg(v, d):
        return jnp.concatenate([pltpu.roll(v[r:r + 8], d, axis=0) for r in range(0, c, 8)], axis=0)

    def chunk(ci, carry):
        r0 = pl.multiple_of(ci * c, c)
        rows = pl.ds(r0, c)
        for hd in range(HGRN_HEADS):
            cols = head_cols[hd]
            lb, oml = head_lb[hd], head_oml[hd]
            q = zq[rows, cols].astype(F32)
            fl = zf[rows, cols].astype(F32)
            iv = zi[rows, cols]
            ef = jnp.exp(-fl)
            sg = 1.0 / (1.0 + ef)
            lf = jnp.log(lb + oml * sg)
            kk = oml * (ef * sg)
            qf = q / (1.0 + jnp.exp(-q))

            b = lf
            for d in (1, 2, 4):
                b = b + jnp.where(sub8 >= d, roll_in_vreg(b, d), 0.0)
            groups = [b[0:8]]
            for r in range(8, c, 8):
                groups.append(b[r:r + 8] + jnp.broadcast_to(groups[-1][7:8], (8, HGRN_DIM)))
            b = jnp.concatenate(groups, axis=0)
            b_sc[hd] = b
            b_last = b_sc[hd, c - 1:c, :]

            qb = qf.astype(BF16)
            kb = kk.astype(BF16)
            scores = mk_ref[len(HGRN_LEVELS)] * jnp.sum(qf * kk, axis=-1, keepdims=True)
            for lev, h in enumerate(HGRN_LEVELS):
                if h == 1:
                    g = jnp.where(odd2, roll_in_vreg(b, 1), b)
                elif h == 2:
                    g = jnp.where(r4 == 0, roll_in_vreg(b, 7),
                                  jnp.where(r4 == 1, b,
                                            jnp.where(r4 == 2, roll_in_vreg(b, 1), roll_in_vreg(b, 2))))
                else:
                    g = jnp.concatenate(
                        [jnp.broadcast_to(b_sc[hd, pl.ds(p0 + h - 1, 1), :], (2 * h, HGRN_DIM))
                         for p0 in range(0, c, 2 * h)], axis=0)
                neg_abs = lax.bitcast_convert_type(
                    lax.bitcast_convert_type(b - g, jnp.uint32) | sign_bit, F32)
                ed = jnp.exp(neg_abs).astype(BF16)
                scores = scores + _dot_nt(qb * ed, kb * ed) * mk_ref[lev]

            st = st_ref[hd]
            o = _dot_nt((qf * jnp.exp(b)).astype(BF16), st.astype(BF16))
            o = o + _dot(scores.astype(BF16), iv)
            kd = (kk * jnp.exp(b_last - b)).astype(BF16)
            st_ref[hd] = jnp.exp(b_last) * st + _dot_tn(iv, kd)

            gt = zg[rows, cols].astype(F32)
            y = _rms_normalize(o, gain) * (gt / (1.0 + jnp.exp(-gt)))
            o_ref[rows, cols] = y.astype(BF16)
        return carry

    lax.fori_loop(0, tc // c, chunk, 0)


def _hgrn(z, hgrn_lb, hgrn_norm, masks, layer, *, tc=512):
    b = z.shape[0]

    def spec(part):
        return pl.BlockSpec((None, tc, HGRN_WIDTH), lambda bi, t: (bi, t, part))

    return pl.pallas_call(
        functools.partial(_hgrn_kernel, layer),
        grid=(b, SEQ // tc),
        in_specs=[spec(0), spec(1), spec(2), spec(3),
                  pl.BlockSpec(hgrn_lb.shape, lambda bi, t: (0, 0)),
                  pl.BlockSpec(hgrn_norm.shape, lambda bi, t: (0, 0)),
                  pl.BlockSpec(masks.shape, lambda bi, t: (0, 0, 0))],
        out_specs=pl.BlockSpec((None, tc, HGRN_WIDTH), lambda bi, t: (bi, t, 0)),
        out_shape=jax.ShapeDtypeStruct((b, SEQ, HGRN_WIDTH), BF16),
        scratch_shapes=[pltpu.VMEM((HGRN_HEADS, HGRN_DIM, HGRN_DIM), F32),
                        pltpu.VMEM((HGRN_HEADS, HGRN_CHUNK, HGRN_DIM), F32)],
        compiler_params=_params(("parallel", "arbitrary"), 40),
        name="hgrn2",
    )(z, z, z, z, hgrn_lb, hgrn_norm, masks)


def _combine_kernel(x_ref, at_ref, hg_ref, gl_ref, bg_ref, wpa_ref, wpm_ref, wo_ref, o_ref, y_ref):
    d = o_ref.shape[1]
    tn = 512
    at = at_ref[...]
    hg = hg_ref[...]
    for c0 in range(0, d, tn):
        ga = jax.nn.sigmoid(gl_ref[:, c0:c0 + tn].astype(F32) + bg_ref[:, c0:c0 + tn])
        gm = jax.nn.sigmoid(gl_ref[:, d + c0:d + c0 + tn].astype(F32) + bg_ref[:, d + c0:d + c0 + tn])
        y = ga * _dot(at, wpa_ref[:, c0:c0 + tn]) + gm * _dot(hg, wpm_ref[:, c0:c0 + tn])
        y_ref[:, c0:c0 + tn] = y.astype(BF16)
    o_ref[...] = x_ref[...] + _dot(y_ref[...], wo_ref[...])


def _combine(x, attn, hgrn, z, b_gate, w_pa, w_pm, w_o, layer, *, tm=512):
    t, d = x.shape

    def rows(width, col_block=0):
        return pl.BlockSpec((tm, width), lambda i: (i, col_block))

    def whole(a):
        return pl.BlockSpec((None,) + a.shape[1:], lambda i: (layer,) + (0,) * (a.ndim - 1),
                            pipeline_mode=pl.Buffered(1))

    return pl.pallas_call(
        _combine_kernel,
        grid=(t // tm,),
        in_specs=[rows(d), rows(GROUP_WIDTH), rows(HGRN_WIDTH), rows(2 * d, Z_GATE // (2 * d)),
                  whole(b_gate), whole(w_pa), whole(w_pm), whole(w_o)],
        out_specs=rows(d),
        out_shape=jax.ShapeDtypeStruct((t, d), F32),
        scratch_shapes=[pltpu.VMEM((tm, d), BF16)],
        compiler_params=_params(("parallel",), 56),
        name="merge_out_proj",
    )(x, attn, hgrn, z, b_gate, w_pa, w_pm, w_o)


def kernel(x, ffn_norm, ffn_w_in, ffn_w_out, mix_norm, w_in, b_gate, hgrn_lb, hgrn_norm,
           w_proj_attn, w_proj_hgrn, w_out, final_norm):
    bsz, seq, d = x.shape
    assert seq == SEQ and d == D_MODEL
    t = bsz * seq

    w_gu, w_gu_tail = _split_ffn_in(ffn_w_in)
    w_down = ffn_w_out.astype(BF16)
    w_mix = _reorder_mix_in(w_in)
    w_pa = w_proj_attn.astype(BF16)
    w_pm = w_proj_hgrn.astype(BF16)
    w_o = w_out.astype(BF16)
    ffn_gain = ffn_norm.reshape(DEPTH, 2, 1, d)
    mix_gain = mix_norm.reshape(DEPTH, 1, d)
    gate_bias = b_gate.reshape(DEPTH, 1, 2 * d)
    masks = jnp.asarray(_hgrn_masks())

    out_gain = final_norm.reshape(1, d)
    xf = x.reshape(t, d)
    for l in range(DEPTH):
        xf = _ffn(xf, ffn_gain, w_gu, w_gu_tail, w_down, out_gain, l, 0)
        z = _norm_matmul(xf, mix_gain, w_mix, l)
        z3 = z.reshape(bsz, seq, Z_WIDTH)
        attn = _attention(z3)
        hgrn = _hgrn(z3, hgrn_lb, hgrn_norm, masks, l)
        xf = _combine(xf, attn.reshape(t, GROUP_WIDTH), hgrn.reshape(t, HGRN_WIDTH), z,
                      gate_bias, w_pa, w_pm, w_o, l)
        xf = _ffn(xf, ffn_gain, w_gu, w_gu_tail, w_down, out_gain, l, 1, norm_output=l == DEPTH - 1)
    return xf.reshape(bsz, seq, d)
```

```python
import functools

import numpy as np
import jax
import jax.numpy as jnp
from jax import lax
from jax.experimental import pallas as pl
from jax.experimental.pallas import tpu as pltpu

F32 = jnp.float32
BF16 = jnp.bfloat16

D_MODEL = 2048
SEQ = 2048
DEPTH = 4
HEAD_DIM = 128
ATTN_HEADS = 4
DILATED_GROUPS = ((128, 1), (512, 4), (2048, 16))
GROUP_WIDTH = ATTN_HEADS * HEAD_DIM
ATTN_WIDTH = len(DILATED_GROUPS) * GROUP_WIDTH
HGRN_HEADS = 8
HGRN_DIM = 128
HGRN_WIDTH = HGRN_HEADS * HGRN_DIM
Z_GATE = 4 * HGRN_WIDTH
Z_ATTN = Z_GATE + 2 * D_MODEL
Z_WIDTH = Z_ATTN + 3 * ATTN_WIDTH
D_FF = 5504
FFN_TILE = 768
FFN_TAIL = 128
MACARON_WEIGHT = 0.5
EPS = 1e-6
MASK_VALUE = -1e30
LANES = 128
ATTN_BLOCK = 128
HGRN_CHUNK = 128
HGRN_LEVELS = (1, 2, 4, 8, 16, 32, 64)
MIB = 1024 * 1024


def _rms_normalize(x, gain):
    ms = jnp.mean(x * x, axis=-1, keepdims=True)
    return x * lax.rsqrt(ms + EPS) * gain


def _dot(a, b):
    return jnp.dot(a, b, preferred_element_type=F32)


def _dot_nt(a, b):
    return lax.dot_general(a, b, (((1,), (1,)), ((), ())), preferred_element_type=F32)


def _dot_tn(a, b):
    return lax.dot_general(a, b, (((0,), (0,)), ((), ())), preferred_element_type=F32)


def _params(semantics, vmem_mib):
    return pltpu.CompilerParams(dimension_semantics=semantics, vmem_limit_bytes=vmem_mib * MIB)


def _split_ffn_in_kernel(src_ref, dst_ref, tail_ref):
    v = src_ref[...].astype(BF16)
    dst_ref[...] = v
    tail_ref[...] = v[:, D_FF - FFN_TAIL:]


def _split_ffn_in(ffn_w_in, *, tr=256):
    depth, two, d, _ = ffn_w_in.shape
    return pl.pallas_call(
        _split_ffn_in_kernel,
        grid=(depth, two, 2, d // tr),
        in_specs=[pl.BlockSpec((None, None, tr, D_FF), lambda l, w, k, r: (l, w, r, k))],
        out_specs=[pl.BlockSpec((None, None, None, tr, D_FF), lambda l, w, k, r: (l, w, k, r, 0)),
                   pl.BlockSpec((None, None, tr, FFN_TAIL), lambda l, w, k, r: (l, w, r, k))],
        out_shape=[jax.ShapeDtypeStruct((depth, two, 2, d, D_FF), BF16),
                   jax.ShapeDtypeStruct((depth, two, d, 2 * FFN_TAIL), BF16)],
        compiler_params=_params(("parallel",) * 4, 32),
        name="split_ffn_in",
    )(ffn_w_in)


def _cast_kernel(src_ref, dst_ref):
    dst_ref[...] = src_ref[...].astype(BF16)


def _reorder_mix_in(w_in, *, tn=512):
    depth, d, n = w_in.shape
    nblk = n // tn
    shift = Z_ATTN // tn
    return pl.pallas_call(
        _cast_kernel,
        grid=(depth, nblk),
        in_specs=[pl.BlockSpec((None, d, tn), lambda l, j: (l, 0, j))],
        out_specs=pl.BlockSpec((None, d, tn), lambda l, j: (l, 0, lax.rem(j + shift, nblk))),
        out_shape=jax.ShapeDtypeStruct((depth, d, n), BF16),
        compiler_params=_params(("parallel", "parallel"), 32),
        name="reorder_mix_in",
    )(w_in)


def _swiglu_act(g, u):
    return (MACARON_WEIGHT * g * jax.nn.sigmoid(g)) * u


def _ffn_kernel(norm_output, x_hbm, gain_ref, wg_ref, wu_ref, wo_ref, wgu_tail_ref, wo_tail_ref,
                out_gain_ref, o_hbm, xo, h_ref, sem_in, sem_out):
    i = pl.program_id(0)
    j = pl.program_id(1)
    n_row_tiles = pl.num_programs(0)
    last_j = pl.num_programs(1) - 1
    tm = xo.shape[1]
    slot = lax.rem(i, 2)
    other = 1 - slot

    def load_tile(tile, s):
        return pltpu.make_async_copy(x_hbm.at[pl.ds(tile * tm, tm), :], xo.at[s], sem_in.at[s])

    def store_tile(tile, s):
        return pltpu.make_async_copy(xo.at[s], o_hbm.at[pl.ds(tile * tm, tm), :], sem_out.at[s])

    @pl.when(j == 0)
    def _():
        @pl.when(i == 0)
        def _():
            load_tile(0, 0).start()

        load_tile(i, slot).wait()
        h_ref[...] = _rms_normalize(xo[slot], gain_ref[...]).astype(BF16)
        gu = jnp.concatenate([_dot(h_ref[r:r + tm // 2, :], wgu_tail_ref[...])
                              for r in (0, tm // 2)], axis=0)
        a = _swiglu_act(gu[:, :FFN_TAIL], gu[:, FFN_TAIL:]).astype(BF16)
        half = xo.shape[2] // 2
        for c0 in (0, half):
            xo[slot, :, c0:c0 + half] += _dot(a, wo_tail_ref[:, c0:c0 + half])

    @pl.when(j == 1)
    def _():
        @pl.when(i > 0)
        def _():
            store_tile(i - 1, other).wait()

        @pl.when(i + 1 < n_row_tiles)
        def _():
            load_tile(i + 1, other).start()

    h = h_ref[...]
    a = _swiglu_act(_dot(h, wg_ref[...]), _dot(h, wu_ref[...]))
    xo[slot] += _dot(a.astype(BF16), wo_ref[...])

    @pl.when(j == last_j)
    def _():
        if norm_output:
            xo[slot] = _rms_normalize(xo[slot], out_gain_ref[...])
        store_tile(i, slot).start()

        @pl.when(i == n_row_tiles - 1)
        def _():
            store_tile(i, slot).wait()


def _ffn(x, gain, w_gu, w_gu_tail, w_down, out_gain, layer, which, *, norm_output=False, tm=1024):
    t, d = x.shape
    tf = FFN_TILE
    n_col_tiles = (D_FF - FFN_TAIL) // tf
    assert t % tm == 0 and n_col_tiles >= 2
    tail_block = (D_FF - FFN_TAIL) // FFN_TAIL
    return pl.pallas_call(
        functools.partial(_ffn_kernel, norm_output),
        grid=(t // tm, n_col_tiles),
        in_specs=[
            pl.BlockSpec(memory_space=pl.ANY),
            pl.BlockSpec((None, None, 1, d), lambda i, j: (layer, which, 0, 0)),
            pl.BlockSpec((None, None, None, d, tf), lambda i, j: (layer, which, 0, 0, j)),
            pl.BlockSpec((None, None, None, d, tf), lambda i, j: (layer, which, 1, 0, j)),
            pl.BlockSpec((None, None, tf, d), lambda i, j: (layer, which, j, 0)),
            pl.BlockSpec((None, None, d, 2 * FFN_TAIL), lambda i, j: (layer, which, 0, 0)),
            pl.BlockSpec((None, None, FFN_TAIL, d), lambda i, j: (layer, which, tail_block, 0)),
            pl.BlockSpec((1, d), lambda i, j: (0, 0)),
        ],
        out_specs=pl.BlockSpec(memory_space=pl.ANY),
        out_shape=jax.ShapeDtypeStruct((t, d), F32),
        scratch_shapes=[pltpu.VMEM((2, tm, d), F32), pltpu.VMEM((tm, d), BF16),
                        pltpu.SemaphoreType.DMA((2,)), pltpu.SemaphoreType.DMA((2,))],
        compiler_params=_params(("arbitrary", "arbitrary"), 58),
        name="ffn",
    )(x, gain, w_gu, w_gu, w_down, w_gu_tail, w_down, out_gain)


def _norm_matmul_kernel(x_ref, gain_ref, w_ref, o_ref, h_ref):
    @pl.when(pl.program_id(1) == 0)
    def _():
        h_ref[...] = _rms_normalize(x_ref[...], gain_ref[...]).astype(BF16)

    o_ref[...] = _dot(h_ref[...], w_ref[...]).astype(o_ref.dtype)


def _norm_matmul(x, gain, w, layer, *, tm=1024, tn=1280):
    t, d = x.shape
    n_cols = w.shape[-1]
    return pl.pallas_call(
        _norm_matmul_kernel,
        grid=(t // tm, n_cols // tn),
        in_specs=[
            pl.BlockSpec((tm, d), lambda i, j: (i, 0)),
            pl.BlockSpec((None, 1, d), lambda i, j: (layer, 0, 0)),
            pl.BlockSpec((None, d, tn), lambda i, j: (layer, 0, j)),
        ],
        out_specs=pl.BlockSpec((tm, tn), lambda i, j: (i, j)),
        out_shape=jax.ShapeDtypeStruct((t, n_cols), BF16),
        scratch_shapes=[pltpu.VMEM((tm, d), BF16)],
        compiler_params=_params(("parallel", "arbitrary"), 48),
        name="in_proj",
    )(x, gain, w)


def _attn_kernel(q0, k0, v0, q1, k1, v1, q2, k2, v2, o_ref,
                 qs, ks, vs, og0, og1, og2, lg0, lg1, lg2):
    nb = ATTN_BLOCK
    scale = HEAD_DIM ** -0.5
    row2 = lax.broadcasted_iota(jnp.int32, (nb, 2 * nb), 0)
    col2 = lax.broadcasted_iota(jnp.int32, (nb, 2 * nb), 1)
    band = (col2 >= row2) & (col2 <= row2 + nb)
    row1 = lax.broadcasted_iota(jnp.int32, (nb, nb), 0)
    col1 = lax.broadcasted_iota(jnp.int32, (nb, nb), 1)
    causal = col1 <= row1

    def block(q, k, v, mask):
        s = _dot_nt(q, k) * scale
        s = jnp.where(mask, s, MASK_VALUE)
        m = jnp.max(s, axis=-1, keepdims=True)
        p = jnp.exp(s - m)
        l = jnp.sum(p, axis=-1, keepdims=True)
        o = _dot(p.astype(BF16), v)
        return o / l, m + jnp.log(l)

    def run_group(dilation, q_ref, k_ref, v_ref, og, lg):
        r = dilation
        nblk = SEQ // (r * nb)
        if r > 1:
            qs[...] = q_ref[...].astype(F32)
            ks[...] = k_ref[...].astype(F32)
            vs[...] = v_ref[...].astype(F32)
        for c in range(r):
            for n in range(nblk):
                first = n == 0
                nk = nb if first else 2 * nb
                q_start = c + r * nb * n
                k_start = q_start if first else q_start - r * nb
                if r == 1:
                    q = q_ref[pl.ds(q_start, nb), :]
                    k = k_ref[pl.ds(k_start, nk), :]
                    v = v_ref[pl.ds(k_start, nk), :]
                    rows = pl.ds(q_start, nb)
                else:
                    q = qs[pl.ds(q_start, nb, stride=r), :].astype(BF16)
                    k = ks[pl.ds(k_start, nk, stride=r), :].astype(BF16)
                    v = vs[pl.ds(k_start, nk, stride=r), :].astype(BF16)
                    rows = pl.ds(q_start, nb, stride=r)
                o, lse = block(q, k, v, causal if first else band)
                og[rows, :] = o
                lg[rows, :] = jnp.broadcast_to(lse, (nb, LANES))

    run_group(DILATED_GROUPS[0][1], q0, k0, v0, og0, lg0)
    run_group(DILATED_GROUPS[1][1], q1, k1, v1, og1, lg1)
    run_group(DILATED_GROUPS[2][1], q2, k2, v2, og2, lg2)

    tile = 256
    for t in range(SEQ // tile):
        rows = pl.ds(t * tile, tile)
        l0, l1, l2 = lg0[rows, :], lg1[rows, :], lg2[rows, :]
        mx = jnp.maximum(jnp.maximum(l0, l1), l2)
        w0, w1, w2 = jnp.exp(l0 - mx), jnp.exp(l1 - mx), jnp.exp(l2 - mx)
        num = w0 * og0[rows, :] + w1 * og1[rows, :] + w2 * og2[rows, :]
        o_ref[rows, :] = (num / (w0 + w1 + w2)).astype(BF16)


def _attention(z):
    b = z.shape[0]
    heads_per_part = ATTN_WIDTH // HEAD_DIM

    def spec(part, group):
        base = Z_ATTN // HEAD_DIM + part * heads_per_part + group * ATTN_HEADS
        return pl.BlockSpec((None, SEQ, HEAD_DIM), lambda bi, h: (bi, 0, base + h))

    in_specs = [spec(part, group) for group in range(3) for part in range(3)]
    seq_f32 = pltpu.VMEM((SEQ, HEAD_DIM), F32)
    return pl.pallas_call(
        _attn_kernel,
        grid=(b, ATTN_HEADS),
        in_specs=in_specs,
        out_specs=pl.BlockSpec((None, SEQ, HEAD_DIM), lambda bi, h: (bi, 0, h)),
        out_shape=jax.ShapeDtypeStruct((b, SEQ, GROUP_WIDTH), BF16),
        scratch_shapes=[seq_f32] * 9,
        compiler_params=_params(("parallel", "parallel"), 40),
        name="dilated_attn",
    )(*([z] * 9))


def _hgrn_masks():
    c = HGRN_CHUNK
    t = np.arange(c)[:, None]
    s = np.arange(c)[None, :]
    masks = []
    for h in HGRN_LEVELS:
        masks.append((t // (2 * h) == s // (2 * h)) & (t % (2 * h) >= h) & (s % (2 * h) < h))
    masks.append(t == s)
    return np.stack(masks).astype(np.float32)


def _hgrn_kernel(layer, zq, zf, zi, zg, lb_ref, gn_ref, mk_ref, o_ref, st_ref, b_sc):
    c = HGRN_CHUNK
    tc = zq.shape[0]

    @pl.when(pl.program_id(1) == 0)
    def _():
        st_ref[...] = jnp.zeros_like(st_ref)

    lb_all = lb_ref[...]
    e = jnp.exp(lb_all - jnp.max(lb_all, axis=0, keepdims=True))
    p = e / jnp.sum(e, axis=0, keepdims=True)
    lower = jnp.zeros((1, HGRN_WIDTH), F32)
    for i in range(1, layer + 1):
        lower = lower + p[i:i + 1, :]
    gain = gn_ref[layer:layer + 1, :]

    rowi = lax.broadcasted_iota(jnp.int32, (c, HGRN_DIM), 0)
    sub8 = rowi % 8
    odd2 = rowi % 2 == 1
    r4 = rowi % 4
    sign_bit = jnp.uint32(0x80000000)
    head_cols = [slice(hd * HGRN_DIM, (hd + 1) * HGRN_DIM) for hd in range(HGRN_HEADS)]
    head_lb = [lower[:, cols] for cols in head_cols]
    head_oml = [1.0 - lb for lb in head_lb]

    def roll_in_vreg(v, d):
        return jnp.concatenate([pltpu.roll(v[r:r + 8], d, axis=0) for r in range(0, c, 8)], axis=0)

    def chunk(ci, carry):
        r0 = pl.multiple_of(ci * c, c)
        rows = pl.ds(r0, c)
        for hd in range(HGRN_HEADS):
            cols = head_cols[hd]
            lb, oml = head_lb[hd], head_oml[hd]
            q = zq[rows, cols].astype(F32)
            fl = zf[rows, cols].astype(F32)
            iv = zi[rows, cols]
            ef = jnp.exp(-fl)
            sg = 1.0 / (1.0 + ef)
            lf = jnp.log(lb + oml * sg)
            kk = oml * (ef * sg)
            qf = q / (1.0 + jnp.exp(-q))

            b = lf
            for d in (1, 2, 4):
                b = b + jnp.where(sub8 >= d, roll_in_vreg(b, d), 0.0)
            groups = [b[0:8]]
            for r in range(8, c, 8):
                groups.append(b[r:r + 8] + jnp.broadcast_to(groups[-1][7:8], (8, HGRN_DIM)))
            b = jnp.concatenate(groups, axis=0)
            b_sc[hd] = b
            b_last = b_sc[hd, c - 1:c, :]

            qb = qf.astype(BF16)
            kb = kk.astype(BF16)
            scores = mk_ref[len(HGRN_LEVELS)] * jnp.sum(qf * kk, axis=-1, keepdims=True)
            for lev, h in enumerate(HGRN_LEVELS):
                if h == 1:
                    g = jnp.where(odd2, roll_in_vreg(b, 1), b)
                elif h == 2:
                    g = jnp.where(r4 == 0, roll_in_vreg(b, 7),
                                  jnp.where(r4 == 1, b,
                                            jnp.where(r4 == 2, roll_in_vreg(b, 1), roll_in_vreg(b, 2))))
                else:
                    g = jnp.concatenate(
                        [jnp.broadcast_to(b_sc[hd, pl.ds(p0 + h - 1, 1), :], (2 * h, HGRN_DIM))
                         for p0 in range(0, c, 2 * h)], axis=0)
                neg_abs = lax.bitcast_convert_type(
                    lax.bitcast_convert_type(b - g, jnp.uint32) | sign_bit, F32)
                ed = jnp.exp(neg_abs).astype(BF16)
                scores = scores + _dot_nt(qb * ed, kb * ed) * mk_ref[lev]

            st = st_ref[hd]
            o = _dot_nt((qf * jnp.exp(b)).astype(BF16), st.astype(BF16))
            o = o + _dot(scores.astype(BF16), iv)
            kd = (kk * jnp.exp(b_last - b)).astype(BF16)
            st_ref[hd] = jnp.exp(b_last) * st + _dot_tn(iv, kd)

            gt = zg[rows, cols].astype(F32)
            y = _rms_normalize(o, gain) * (gt / (1.0 + jnp.exp(-gt)))
            o_ref[rows, cols] = y.astype(BF16)
        return carry

    lax.fori_loop(0, tc // c, chunk, 0)


def _hgrn(z, hgrn_lb, hgrn_norm, masks, layer, *, tc=512):
    b = z.shape[0]

    def spec(part):
        return pl.BlockSpec((None, tc, HGRN_WIDTH), lambda bi, t: (bi, t, part))

    return pl.pallas_call(
        functools.partial(_hgrn_kernel, layer),
        grid=(b, SEQ // tc),
        in_specs=[spec(0), spec(1), spec(2), spec(3),
                  pl.BlockSpec(hgrn_lb.shape, lambda bi, t: (0, 0)),
                  pl.BlockSpec(hgrn_norm.shape, lambda bi, t: (0, 0)),
                  pl.BlockSpec(masks.shape, lambda bi, t: (0, 0, 0))],
        out_specs=pl.BlockSpec((None, tc, HGRN_WIDTH), lambda bi, t: (bi, t, 0)),
        out_shape=jax.ShapeDtypeStruct((b, SEQ, HGRN_WIDTH), BF16),
        scratch_shapes=[pltpu.VMEM((HGRN_HEADS, HGRN_DIM, HGRN_DIM), F32),
                        pltpu.VMEM((HGRN_HEADS, HGRN_CHUNK, HGRN_DIM), F32)],
        compiler_params=_params(("parallel", "arbitrary"), 40),
        name="hgrn2",
    )(z, z, z, z, hgrn_lb, hgrn_norm, masks)


def _combine_kernel(x_ref, at_ref, hg_ref, gl_ref, bg_ref, wpa_ref, wpm_ref, wo_ref, o_ref, y_ref):
    d = o_ref.shape[1]
    tn = 512
    at = at_ref[...]
    hg = hg_ref[...]
    for c0 in range(0, d, tn):
        ga = jax.nn.sigmoid(gl_ref[:, c0:c0 + tn].astype(F32) + bg_ref[:, c0:c0 + tn])
        gm = jax.nn.sigmoid(gl_ref[:, d + c0:d + c0 + tn].astype(F32) + bg_ref[:, d + c0:d + c0 + tn])
        y = ga * _dot(at, wpa_ref[:, c0:c0 + tn]) + gm * _dot(hg, wpm_ref[:, c0:c0 + tn])
        y_ref[:, c0:c0 + tn] = y.astype(BF16)
    o_ref[...] = x_ref[...] + _dot(y_ref[...], wo_ref[...])


def _combine(x, attn, hgrn, z, b_gate, w_pa, w_pm, w_o, layer, *, tm=512):
    t, d = x.shape

    def rows(width, col_block=0):
        return pl.BlockSpec((tm, width), lambda i: (i, col_block))

    def whole(a):
        return pl.BlockSpec((None,) + a.shape[1:], lambda i: (layer,) + (0,) * (a.ndim - 1),
                            pipeline_mode=pl.Buffered(1))

    return pl.pallas_call(
        _combine_kernel,
        grid=(t // tm,),
        in_specs=[rows(d), rows(GROUP_WIDTH), rows(HGRN_WIDTH), rows(2 * d, Z_GATE // (2 * d)),
                  whole(b_gate), whole(w_pa), whole(w_pm), whole(w_o)],
        out_specs=rows(d),
        out_shape=jax.ShapeDtypeStruct((t, d), F32),
        scratch_shapes=[pltpu.VMEM((tm, d), BF16)],
        compiler_params=_params(("parallel",), 56),
        name="merge_out_proj",
    )(x, attn, hgrn, z, b_gate, w_pa, w_pm, w_o)


def kernel(x, ffn_norm, ffn_w_in, ffn_w_out, mix_norm, w_in, b_gate, hgrn_lb, hgrn_norm,
           w_proj_attn, w_proj_hgrn, w_out, final_norm):
    bsz, seq, d = x.shape
    assert seq == SEQ and d == D_MODEL
    t = bsz * seq

    w_gu, w_gu_tail = _split_ffn_in(ffn_w_in)
    w_down = ffn_w_out.astype(BF16)
    w_mix = _reorder_mix_in(w_in)
    w_pa = w_proj_attn.astype(BF16)
    w_pm = w_proj_hgrn.astype(BF16)
    w_o = w_out.astype(BF16)
    ffn_gain = ffn_norm.reshape(DEPTH, 2, 1, d)
    mix_gain = mix_norm.reshape(DEPTH, 1, d)
    gate_bias = b_gate.reshape(DEPTH, 1, 2 * d)
    masks = jnp.asarray(_hgrn_masks())

    out_gain = final_norm.reshape(1, d)
    xf = x.reshape(t, d)
    for l in range(DEPTH):
        xf = _ffn(xf, ffn_gain, w_gu, w_gu_tail, w_down, out_gain, l, 0)
        z = _norm_matmul(xf, mix_gain, w_mix, l)
        z3 = z.reshape(bsz, seq, Z_WIDTH)
        attn = _attention(z3)
        hgrn = _hgrn(z3, hgrn_lb, hgrn_norm, masks, l)
        xf = _combine(xf, attn.reshape(t, GROUP_WIDTH), hgrn.reshape(t, HGRN_WIDTH), z,
                      gate_bias, w_pa, w_pm, w_o, l)
        xf = _ffn(xf, ffn_gain, w_gu, w_gu_tail, w_down, out_gain, l, 1, norm_output=l == DEPTH - 1)
    return xf.reshape(bsz, seq, d)
```

```python
import functools

import numpy as np
import jax
import jax.numpy as jnp
from jax import lax
from jax.experimental import pallas as pl
from jax.experimental.pallas import tpu as pltpu

F32 = jnp.float32
BF16 = jnp.bfloat16

D_MODEL = 2048
SEQ = 2048
DEPTH = 4
HEAD_DIM = 128
ATTN_HEADS = 4
DILATED_GROUPS = ((128, 1), (512, 4), (2048, 16))
GROUP_WIDTH = ATTN_HEADS * HEAD_DIM
ATTN_WIDTH = len(DILATED_GROUPS) * GROUP_WIDTH
HGRN_HEADS = 8
HGRN_DIM = 128
HGRN_WIDTH = HGRN_HEADS * HGRN_DIM
Z_GATE = 4 * HGRN_WIDTH
Z_ATTN = Z_GATE + 2 * D_MODEL
Z_WIDTH = Z_ATTN + 3 * ATTN_WIDTH
D_FF = 5504
FFN_TILE = 768
FFN_TAIL = 128
MACARON_WEIGHT = 0.5
EPS = 1e-6
MASK_VALUE = -1e30
LANES = 128
ATTN_BLOCK = 128
HGRN_CHUNK = 128
HGRN_LEVELS = (1, 2, 4, 8, 16, 32, 64)
MIB = 1024 * 1024


def _rms_normalize(x, gain):
    ms = jnp.mean(x * x, axis=-1, keepdims=True)
    return x * lax.rsqrt(ms + EPS) * gain


def _dot(a, b):
    return jnp.dot(a, b, preferred_element_type=F32)


def _dot_nt(a, b):
    return lax.dot_general(a, b, (((1,), (1,)), ((), ())), preferred_element_type=F32)


def _dot_tn(a, b):
    return lax.dot_general(a, b, (((0,), (0,)), ((), ())), preferred_element_type=F32)


def _params(semantics, vmem_mib):
    return pltpu.CompilerParams(dimension_semantics=semantics, vmem_limit_bytes=vmem_mib * MIB)


def _split_ffn_in_kernel(src_ref, dst_ref, tail_ref):
    v = src_ref[...].astype(BF16)
    dst_ref[...] = v
    tail_ref[...] = v[:, D_FF - FFN_TAIL:]


def _split_ffn_in(ffn_w_in, *, tr=256):
    depth, two, d, _ = ffn_w_in.shape
    return pl.pallas_call(
        _split_ffn_in_kernel,
        grid=(depth, two, 2, d // tr),
        in_specs=[pl.BlockSpec((None, None, tr, D_FF), lambda l, w, k, r: (l, w, r, k))],
        out_specs=[pl.BlockSpec((None, None, None, tr, D_FF), lambda l, w, k, r: (l, w, k, r, 0)),
                   pl.BlockSpec((None, None, tr, FFN_TAIL), lambda l, w, k, r: (l, w, r, k))],
        out_shape=[jax.ShapeDtypeStruct((depth, two, 2, d, D_FF), BF16),
                   jax.ShapeDtypeStruct((depth, two, d, 2 * FFN_TAIL), BF16)],
        compiler_params=_params(("parallel",) * 4, 32),
        name="split_ffn_in",
    )(ffn_w_in)


def _reorder_mix_in_kernel(n_query_blocks, src_ref, dst_ref):
    c = jnp.where(pl.program_id(1) < n_query_blocks, HEAD_DIM ** -0.5, 1.0)
    dst_ref[...] = (src_ref[...] * c).astype(BF16)


def _reorder_mix_in(w_in, *, tn=512):
    depth, d, n = w_in.shape
    nblk = n // tn
    shift = Z_ATTN // tn
    assert ATTN_WIDTH % tn == 0
    return pl.pallas_call(
        functools.partial(_reorder_mix_in_kernel, ATTN_WIDTH // tn),
        grid=(depth, nblk),
        in_specs=[pl.BlockSpec((None, d, tn), lambda l, j: (l, 0, j))],
        out_specs=pl.BlockSpec((None, d, tn), lambda l, j: (l, 0, lax.rem(j + shift, nblk))),
        out_shape=jax.ShapeDtypeStruct((depth, d, n), BF16),
        compiler_params=_params(("parallel", "parallel"), 32),
        name="reorder_mix_in",
    )(w_in)


def _swiglu_act(g, u):
    return (MACARON_WEIGHT * g * jax.nn.sigmoid(g)) * u


def _ffn_kernel(norm_output, x_hbm, gain_ref, wg_ref, wu_ref, wo_ref, wgu_tail_ref, wo_tail_ref,
                out_gain_ref, o_hbm, xo, h_ref, sem_in, sem_out):
    i = pl.program_id(0)
    j = pl.program_id(1)
    n_row_tiles = pl.num_programs(0)
    last_j = pl.num_programs(1) - 1
    tm = xo.shape[1]
    slot = lax.rem(i, 2)
    other = 1 - slot

    def load_tile(tile, s):
        return pltpu.make_async_copy(x_hbm.at[pl.ds(tile * tm, tm), :], xo.at[s], sem_in.at[s])

    def store_tile(tile, s):
        return pltpu.make_async_copy(xo.at[s], o_hbm.at[pl.ds(tile * tm, tm), :], sem_out.at[s])

    @pl.when(j == 0)
    def _():
        @pl.when(i == 0)
        def _():
            load_tile(0, 0).start()

        load_tile(i, slot).wait()
        h_ref[...] = _rms_normalize(xo[slot], gain_ref[...]).astype(BF16)
        gu = jnp.concatenate([_dot(h_ref[r:r + tm // 2, :], wgu_tail_ref[...])
                              for r in (0, tm // 2)], axis=0)
        a = _swiglu_act(gu[:, :FFN_TAIL], gu[:, FFN_TAIL:]).astype(BF16)
        half = xo.shape[2] // 2
        for c0 in (0, half):
            xo[slot, :, c0:c0 + half] += _dot(a, wo_tail_ref[:, c0:c0 + half])

    @pl.when(j == 1)
    def _():
        @pl.when(i > 0)
        def _():
            store_tile(i - 1, other).wait()

        @pl.when(i + 1 < n_row_tiles)
        def _():
            load_tile(i + 1, other).start()

    h = h_ref[...]
    a = _swiglu_act(_dot(h, wg_ref[...]), _dot(h, wu_ref[...]))
    xo[slot] += _dot(a.astype(BF16), wo_ref[...])

    @pl.when(j == last_j)
    def _():
        if norm_output:
            xo[slot] = _rms_normalize(xo[slot], out_gain_ref[...])
        store_tile(i, slot).start()

        @pl.when(i == n_row_tiles - 1)
        def _():
            store_tile(i, slot).wait()


def _ffn(x, gain, w_gu, w_gu_tail, w_down, out_gain, layer, which, *, norm_output=False, tm=1024):
    t, d = x.shape
    tf = FFN_TILE
    n_col_tiles = (D_FF - FFN_TAIL) // tf
    assert t % tm == 0 and n_col_tiles >= 2
    tail_block = (D_FF - FFN_TAIL) // FFN_TAIL
    return pl.pallas_call(
        functools.partial(_ffn_kernel, norm_output),
        grid=(t // tm, n_col_tiles),
        in_specs=[
            pl.BlockSpec(memory_space=pl.ANY),
            pl.BlockSpec((None, None, 1, d), lambda i, j: (layer, which, 0, 0)),
            pl.BlockSpec((None, None, None, d, tf), lambda i, j: (layer, which, 0, 0, j)),
            pl.BlockSpec((None, None, None, d, tf), lambda i, j: (layer, which, 1, 0, j)),
            pl.BlockSpec((None, None, tf, d), lambda i, j: (layer, which, j, 0)),
            pl.BlockSpec((None, None, d, 2 * FFN_TAIL), lambda i, j: (layer, which, 0, 0)),
            pl.BlockSpec((None, None, FFN_TAIL, d), lambda i, j: (layer, which, tail_block, 0)),
            pl.BlockSpec((1, d), lambda i, j: (0, 0)),
        ],
        out_specs=pl.BlockSpec(memory_space=pl.ANY),
        out_shape=jax.ShapeDtypeStruct((t, d), F32),
        scratch_shapes=[pltpu.VMEM((2, tm, d), F32), pltpu.VMEM((tm, d), BF16),
                        pltpu.SemaphoreType.DMA((2,)), pltpu.SemaphoreType.DMA((2,))],
        compiler_params=_params(("arbitrary", "arbitrary"), 58),
        name="ffn",
    )(x, gain, w_gu, w_gu, w_down, w_gu_tail, w_down, out_gain)


def _norm_matmul_kernel(x_ref, gain_ref, w_ref, o_ref, h_ref):
    @pl.when(pl.program_id(1) == 0)
    def _():
        h_ref[...] = _rms_normalize(x_ref[...], gain_ref[...]).astype(BF16)

    o_ref[...] = _dot(h_ref[...], w_ref[...]).astype(o_ref.dtype)


def _norm_matmul(x, gain, w, layer, *, tm=1024, tn=1280):
    t, d = x.shape
    n_cols = w.shape[-1]
    return pl.pallas_call(
        _norm_matmul_kernel,
        grid=(t // tm, n_cols // tn),
        in_specs=[
            pl.BlockSpec((tm, d), lambda i, j: (i, 0)),
            pl.BlockSpec((None, 1, d), lambda i, j: (layer, 0, 0)),
            pl.BlockSpec((None, d, tn), lambda i, j: (layer, 0, j)),
        ],
        out_specs=pl.BlockSpec((tm, tn), lambda i, j: (i, j)),
        out_shape=jax.ShapeDtypeStruct((t, n_cols), BF16),
        scratch_shapes=[pltpu.VMEM((tm, d), BF16)],
        compiler_params=_params(("parallel", "arbitrary"), 48),
        name="in_proj",
    )(x, gain, w)


def _attn_kernel(q0, k0, v0, q1, k1, v1, q2, k2, v2, o_ref,
                 qs1, ks1, vs1, qs2, ks2, vs2, og0, og1, og2, lg0, lg1, lg2):
    nb = ATTN_BLOCK
    row2 = lax.broadcasted_iota(jnp.int32, (nb, 2 * nb), 0)
    col2 = lax.broadcasted_iota(jnp.int32, (nb, 2 * nb), 1)
    band = (col2 >= row2) & (col2 <= row2 + nb)
    row1 = lax.broadcasted_iota(jnp.int32, (nb, nb), 0)
    col1 = lax.broadcasted_iota(jnp.int32, (nb, nb), 1)
    causal = col1 <= row1

    ones = jnp.ones((2 * nb, HEAD_DIM), BF16)

    def block(q, k, v, mask):
        s = jnp.where(mask, _dot_nt(q, k), MASK_VALUE)
        m = jnp.max(s, axis=-1, keepdims=True)
        p = jnp.exp(s - m).astype(BF16)
        ov = _dot(p, jnp.concatenate([v, ones[:v.shape[0]]], axis=1))
        o, l = ov[:, :HEAD_DIM], ov[:, HEAD_DIM:]
        return o / l, m + jnp.log(l)

    def run_group(dilation, q_ref, k_ref, v_ref, og, lg, qs=None, ks=None, vs=None):
        r = dilation
        nblk = SEQ // (r * nb)
        if r > 1:
            qs[...] = q_ref[...].astype(F32)
            ks[...] = k_ref[...].astype(F32)
            vs[...] = v_ref[...].astype(F32)
        for c in range(r):
            for n in range(nblk):
                first = n == 0
                nk = nb if first else 2 * nb
                q_start = c + r * nb * n
                k_start = q_start if first else q_start - r * nb
                if r == 1:
                    q = q_ref[pl.ds(q_start, nb), :]
                    k = k_ref[pl.ds(k_start, nk), :]
                    v = v_ref[pl.ds(k_start, nk), :]
                    rows = pl.ds(q_start, nb)
                else:
                    q = qs[pl.ds(q_start, nb, stride=r), :].astype(BF16)
                    k = ks[pl.ds(k_start, nk, stride=r), :].astype(BF16)
                    v = vs[pl.ds(k_start, nk, stride=r), :].astype(BF16)
                    rows = pl.ds(q_start, nb, stride=r)
                o, lse = block(q, k, v, causal if first else band)
                og[rows, :] = o
                lg[rows, :] = lse

    run_group(DILATED_GROUPS[0][1], q0, k0, v0, og0, lg0)
    run_group(DILATED_GROUPS[1][1], q1, k1, v1, og1, lg1, qs1, ks1, vs1)
    run_group(DILATED_GROUPS[2][1], q2, k2, v2, og2, lg2, qs2, ks2, vs2)

    tile = 256
    for t in range(SEQ // tile):
        rows = pl.ds(t * tile, tile)
        l0, l1, l2 = lg0[rows, :], lg1[rows, :], lg2[rows, :]
        mx = jnp.maximum(jnp.maximum(l0, l1), l2)
        w0, w1, w2 = jnp.exp(l0 - mx), jnp.exp(l1 - mx), jnp.exp(l2 - mx)
        num = w0 * og0[rows, :] + w1 * og1[rows, :] + w2 * og2[rows, :]
        o_ref[rows, :] = (num / (w0 + w1 + w2)).astype(BF16)


def _attention(z):
    b = z.shape[0]
    heads_per_part = ATTN_WIDTH // HEAD_DIM

    def spec(part, group):
        base = Z_ATTN // HEAD_DIM + part * heads_per_part + group * ATTN_HEADS
        return pl.BlockSpec((None, SEQ, HEAD_DIM), lambda bi, h: (bi, 0, base + h))

    in_specs = [spec(part, group) for group in range(3) for part in range(3)]
    seq_f32 = pltpu.VMEM((SEQ, HEAD_DIM), F32)
    return pl.pallas_call(
        _attn_kernel,
        grid=(b, ATTN_HEADS),
        in_specs=in_specs,
        out_specs=pl.BlockSpec((None, SEQ, HEAD_DIM), lambda bi, h: (bi, 0, h)),
        out_shape=jax.ShapeDtypeStruct((b, SEQ, GROUP_WIDTH), BF16),
        scratch_shapes=[seq_f32] * 12,
        compiler_params=_params(("parallel", "parallel"), 40),
        name="dilated_attn",
    )(*([z] * 9))


def _hgrn_masks():
    c = HGRN_CHUNK
    t = np.arange(c)[:, None]
    s = np.arange(c)[None, :]
    masks = []
    for h in HGRN_LEVELS:
        masks.append((t // (2 * h) == s // (2 * h)) & (t % (2 * h) >= h) & (s % (2 * h) < h))
    masks.append(t == s)
    return np.stack(masks).astype(np.float32)


def _hgrn_kernel(layer, zq, zf, zi, zg, lb_ref, gn_ref, mk_ref, o_ref, st_ref, b_sc):
    c = HGRN_CHUNK
    tc = zq.shape[0]

    @pl.when(pl.program_id(1) == 0)
    def _():
        st_ref[...] = jnp.zeros_like(st_ref)

    lb_all = lb_ref[...]
    e = jnp.exp(lb_all - jnp.max(lb_all, axis=0, keepdims=True))
    p = e / jnp.sum(e, axis=0, keepdims=True)
    lower = jnp.zeros((1, HGRN_WIDTH), F32)
    for i in range(1, layer + 1):
        lower = lower + p[i:i + 1, :]
    gain = gn_ref[layer:layer + 1, :]

    rowi = lax.broadcasted_iota(jnp.int32, (c, HGRN_DIM), 0)
    sub8 = rowi % 8
    odd2 = rowi % 2 == 1
    r4 = rowi % 4
    sign_bit = jnp.uint32(0x80000000)
    head_cols = [slice(hd * HGRN_DIM, (hd + 1) * HGRN_DIM) for hd in range(HGRN_HEADS)]
    head_lb = [lower[:, cols] for cols in head_cols]
    head_oml = [1.0 - lb for lb in head_lb]

    def roll_in_vreg(v, d):
        return jnp.concatenate([pltpu.roll(v[r:r + 8], d, axis=0) for r in range(0, c, 8)], axis=0)

    def chunk(ci, carry):
        rows = pl.ds(pl.multiple_of(ci * c, c), c)
        for hd in range(HGRN_HEADS):
            cols = head_cols[hd]
            lb, oml = head_lb[hd], head_oml[hd]
            q = zq[rows, cols].astype(F32)
            fl = zf[rows, cols].astype(F32)
            iv = zi[rows, cols]
            ef = jnp.exp(-fl)
            sg = 1.0 / (1.0 + ef)
            lf = jnp.log(lb + oml * sg)
            kk = oml * (ef * sg)
            qf = q / (1.0 + jnp.exp(-q))

            b = lf
            for d in (1, 2, 4):
                b = b + jnp.where(sub8 >= d, roll_in_vreg(b, d), 0.0)
            groups = [b[0:8]]
            for r in range(8, c, 8):
                groups.append(b[r:r + 8] + jnp.broadcast_to(groups[-1][7:8], (8, HGRN_DIM)))
            b = jnp.concatenate(groups, axis=0)
            b_sc[hd] = b
            b_last = b_sc[hd, c - 1:c, :]

            qb = qf.astype(BF16)
            kb = kk.astype(BF16)
            scores = _dot_nt(qb, kb) * mk_ref[len(HGRN_LEVELS)]
            for lev, h in enumerate(HGRN_LEVELS):
                if h == 1:
                    g = jnp.where(odd2, roll_in_vreg(b, 1), b)
                elif h == 2:
                    g = jnp.where(r4 == 0, roll_in_vreg(b, 7),
                                  jnp.where(r4 == 1, b,
                                            jnp.where(r4 == 2, roll_in_vreg(b, 1), roll_in_vreg(b, 2))))
                else:
                    g = jnp.concatenate(
                        [jnp.broadcast_to(b_sc[hd, pl.ds(p0 + h - 1, 1), :], (2 * h, HGRN_DIM))
                         for p0 in range(0, c, 2 * h)], axis=0)
                neg_abs = lax.bitcast_convert_type(
                    lax.bitcast_convert_type(b - g, jnp.uint32) | sign_bit, F32)
                ed = jnp.exp(neg_abs).astype(BF16)
                scores = scores + _dot_nt(qb * ed, kb * ed) * mk_ref[lev]

            st = st_ref[hd]
            o = _dot_nt((qf * jnp.exp(b)).astype(BF16), st.astype(BF16))
            o = o + _dot(scores.astype(BF16), iv)
            kd = (kk * jnp.exp(b_last - b)).astype(BF16)
            st_ref[hd] = jnp.exp(b_last) * st + _dot_tn(iv, kd)

            gt = zg[rows, cols].astype(F32)
            y = _rms_normalize(o, gain) * (gt / (1.0 + jnp.exp(-gt)))
            o_ref[rows, cols] = y.astype(BF16)
        return carry

    lax.fori_loop(0, tc // c, chunk, 0)


def _hgrn(z, hgrn_lb, hgrn_norm, masks, layer, *, tc=512):
    b = z.shape[0]

    def spec(part):
        return pl.BlockSpec((None, tc, HGRN_WIDTH), lambda bi, t: (bi, t, part))

    return pl.pallas_call(
        functools.partial(_hgrn_kernel, layer),
        grid=(b, SEQ // tc),
        in_specs=[spec(0), spec(1), spec(2), spec(3),
                  pl.BlockSpec(hgrn_lb.shape, lambda bi, t: (0, 0)),
                  pl.BlockSpec(hgrn_norm.shape, lambda bi, t: (0, 0)),
                  pl.BlockSpec(masks.shape, lambda bi, t: (0, 0, 0))],
        out_specs=pl.BlockSpec((None, tc, HGRN_WIDTH), lambda bi, t: (bi, t, 0)),
        out_shape=jax.ShapeDtypeStruct((b, SEQ, HGRN_WIDTH), BF16),
        scratch_shapes=[pltpu.VMEM((HGRN_HEADS, HGRN_DIM, HGRN_DIM), F32),
                        pltpu.VMEM((HGRN_HEADS, HGRN_CHUNK, HGRN_DIM), F32)],
        compiler_params=_params(("parallel", "arbitrary"), 40),
        name="hgrn2",
    )(z, z, z, z, hgrn_lb, hgrn_norm, masks)


def _combine_kernel(x_ref, at_ref, hg_ref, gl_ref, bg_ref, wpa_ref, wpm_ref, wo_ref, o_ref, y_ref):
    d = o_ref.shape[1]
    tn = 512
    at = at_ref[...]
    hg = hg_ref[...]
    for c0 in range(0, d, tn):
        ga = jax.nn.sigmoid(gl_ref[:, c0:c0 + tn].astype(F32) + bg_ref[:, c0:c0 + tn])
        gm = jax.nn.sigmoid(gl_ref[:, d + c0:d + c0 + tn].astype(F32) + bg_ref[:, d + c0:d + c0 + tn])
        y = ga * _dot(at, wpa_ref[:, c0:c0 + tn]) + gm * _dot(hg, wpm_ref[:, c0:c0 + tn])
        y_ref[:, c0:c0 + tn] = y.astype(BF16)
    o_ref[...] = x_ref[...] + _dot(y_ref[...], wo_ref[...])


def _combine(x, attn, hgrn, z, b_gate, w_pa, w_pm, w_o, layer, *, tm=512):
    t, d = x.shape

    def rows(width, col_block=0):
        return pl.BlockSpec((tm, width), lambda i: (i, col_block))

    def whole(a):
        return pl.BlockSpec((None,) + a.shape[1:], lambda i: (layer,) + (0,) * (a.ndim - 1),
                            pipeline_mode=pl.Buffered(1))

    return pl.pallas_call(
        _combine_kernel,
        grid=(t // tm,),
        in_specs=[rows(d), rows(GROUP_WIDTH), rows(HGRN_WIDTH), rows(2 * d, Z_GATE // (2 * d)),
                  whole(b_gate), whole(w_pa), whole(w_pm), whole(w_o)],
        out_specs=rows(d),
        out_shape=jax.ShapeDtypeStruct((t, d), F32),
        scratch_shapes=[pltpu.VMEM((tm, d), BF16)],
        compiler_params=_params(("parallel",), 56),
        name="merge_out_proj",
    )(x, attn, hgrn, z, b_gate, w_pa, w_pm, w_o)


def kernel(x, ffn_norm, ffn_w_in, ffn_w_out, mix_norm, w_in, b_gate, hgrn_lb, hgrn_norm,
           w_proj_attn, w_proj_hgrn, w_out, final_norm):
    bsz, seq, d = x.shape
    assert seq == SEQ and d == D_MODEL
    t = bsz * seq

    w_gu, w_gu_tail = _split_ffn_in(ffn_w_in)
    w_down = ffn_w_out.astype(BF16)
    w_mix = _reorder_mix_in(w_in)
    w_pa = w_proj_attn.astype(BF16)
    w_pm = w_proj_hgrn.astype(BF16)
    w_o = w_out.astype(BF16)
    ffn_gain = ffn_norm.reshape(DEPTH, 2, 1, d)
    mix_gain = mix_norm.reshape(DEPTH, 1, d)
    gate_bias = b_gate.reshape(DEPTH, 1, 2 * d)
    masks = jnp.asarray(_hgrn_masks())

    out_gain = final_norm.reshape(1, d)
    xf = x.reshape(t, d)
    for l in range(DEPTH):
        xf = _ffn(xf, ffn_gain, w_gu, w_gu_tail, w_down, out_gain, l, 0)
        z = _norm_matmul(xf, mix_gain, w_mix, l)
        z3 = z.reshape(bsz, seq, Z_WIDTH)
        attn = _attention(z3)
        hgrn = _hgrn(z3, hgrn_lb, hgrn_norm, masks, l)
        xf = _combine(xf, attn.reshape(t, GROUP_WIDTH), hgrn.reshape(t, HGRN_WIDTH), z,
                      gate_bias, w_pa, w_pm, w_o, l)
        xf = _ffn(xf, ffn_gain, w_gu, w_gu_tail, w_down, out_gain, l, 1, norm_output=l == DEPTH - 1)
    return xf.reshape(bsz, seq, d)
```

```python
import functools

import numpy as np
import jax
import jax.numpy as jnp
from jax import lax
from jax.experimental import pallas as pl
from jax.experimental.pallas import tpu as pltpu

F32 = jnp.float32
BF16 = jnp.bfloat16

D_MODEL = 2048
SEQ = 2048
DEPTH = 4
HEAD_DIM = 128
ATTN_HEADS = 4
DILATED_GROUPS = ((128, 1), (512, 4), (2048, 16))
GROUP_WIDTH = ATTN_HEADS * HEAD_DIM
ATTN_WIDTH = len(DILATED_GROUPS) * GROUP_WIDTH
HGRN_HEADS = 8
HGRN_DIM = 128
HGRN_WIDTH = HGRN_HEADS * HGRN_DIM
Z_GATE = 4 * HGRN_WIDTH
Z_ATTN = Z_GATE + 2 * D_MODEL
Z_WIDTH = Z_ATTN + 3 * ATTN_WIDTH
D_FF = 5504
FFN_TILE = 768
FFN_TAIL = 128
MACARON_WEIGHT = 0.5
EPS = 1e-6
MASK_VALUE = -1e30
LANES = 128
ATTN_BLOCK = 128
HGRN_CHUNK = 128
HGRN_LEVELS = (1, 2, 4, 8, 16, 32, 64)
MIB = 1024 * 1024


def _rms_normalize(x, gain):
    ms = jnp.mean(x * x, axis=-1, keepdims=True)
    return x * lax.rsqrt(ms + EPS) * gain


def _dot(a, b):
    return jnp.dot(a, b, preferred_element_type=F32)


def _dot_nt(a, b):
    return lax.dot_general(a, b, (((1,), (1,)), ((), ())), preferred_element_type=F32)


def _dot_tn(a, b):
    return lax.dot_general(a, b, (((0,), (0,)), ((), ())), preferred_element_type=F32)


def _params(semantics, vmem_mib):
    return pltpu.CompilerParams(dimension_semantics=semantics, vmem_limit_bytes=vmem_mib * MIB)


def _split_ffn_in_kernel(src_ref, dst_ref, tail_ref):
    v = src_ref[...].astype(BF16)
    dst_ref[...] = v
    tail_ref[...] = v[:, D_FF - FFN_TAIL:]


def _split_ffn_in(ffn_w_in, *, tr=256):
    depth, two, d, _ = ffn_w_in.shape
    return pl.pallas_call(
        _split_ffn_in_kernel,
        grid=(depth, two, 2, d // tr),
        in_specs=[pl.BlockSpec((None, None, tr, D_FF), lambda l, w, k, r: (l, w, r, k))],
        out_specs=[pl.BlockSpec((None, None, None, tr, D_FF), lambda l, w, k, r: (l, w, k, r, 0)),
                   pl.BlockSpec((None, None, tr, FFN_TAIL), lambda l, w, k, r: (l, w, r, k))],
        out_shape=[jax.ShapeDtypeStruct((depth, two, 2, d, D_FF), BF16),
                   jax.ShapeDtypeStruct((depth, two, d, 2 * FFN_TAIL), BF16)],
        compiler_params=_params(("parallel",) * 4, 32),
        name="split_ffn_in",
    )(ffn_w_in)


def _reorder_mix_in_kernel(n_query_blocks, src_ref, dst_ref):
    c = jnp.where(pl.program_id(1) < n_query_blocks, HEAD_DIM ** -0.5, 1.0)
    dst_ref[...] = (src_ref[...] * c).astype(BF16)


def _reorder_mix_in(w_in, *, tn=512):
    depth, d, n = w_in.shape
    nblk = n // tn
    shift = Z_ATTN // tn
    assert ATTN_WIDTH % tn == 0
    return pl.pallas_call(
        functools.partial(_reorder_mix_in_kernel, ATTN_WIDTH // tn),
        grid=(depth, nblk),
        in_specs=[pl.BlockSpec((None, d, tn), lambda l, j: (l, 0, j))],
        out_specs=pl.BlockSpec((None, d, tn), lambda l, j: (l, 0, lax.rem(j + shift, nblk))),
        out_shape=jax.ShapeDtypeStruct((depth, d, n), BF16),
        compiler_params=_params(("parallel", "parallel"), 32),
        name="reorder_mix_in",
    )(w_in)


def _swiglu_act(g, u):
    return (MACARON_WEIGHT * g * jax.nn.sigmoid(g)) * u


def _ffn_kernel(norm_output, x_hbm, gain_ref, wg_ref, wu_ref, wo_ref, wgu_tail_ref, wo_tail_ref,
                out_gain_ref, o_hbm, xo, h_ref, sem_in, sem_out):
    i = pl.program_id(0)
    j = pl.program_id(1)
    n_row_tiles = pl.num_programs(0)
    last_j = pl.num_programs(1) - 1
    tm = xo.shape[1]
    slot = lax.rem(i, 2)
    other = 1 - slot

    def load_tile(tile, s):
        return pltpu.make_async_copy(x_hbm.at[pl.ds(tile * tm, tm), :], xo.at[s], sem_in.at[s])

    def store_tile(tile, s):
        return pltpu.make_async_copy(xo.at[s], o_hbm.at[pl.ds(tile * tm, tm), :], sem_out.at[s])

    @pl.when(j == 0)
    def _():
        @pl.when(i == 0)
        def _():
            load_tile(0, 0).start()

        load_tile(i, slot).wait()
        h_ref[...] = _rms_normalize(xo[slot], gain_ref[...]).astype(BF16)
        gu = jnp.concatenate([_dot(h_ref[r:r + tm // 2, :], wgu_tail_ref[...])
                              for r in (0, tm // 2)], axis=0)
        a = _swiglu_act(gu[:, :FFN_TAIL], gu[:, FFN_TAIL:]).astype(BF16)
        half = xo.shape[2] // 2
        for c0 in (0, half):
            xo[slot, :, c0:c0 + half] += _dot(a, wo_tail_ref[:, c0:c0 + half].astype(BF16))

    @pl.when(j == 1)
    def _():
        @pl.when(i > 0)
        def _():
            store_tile(i - 1, other).wait()

        @pl.when(i + 1 < n_row_tiles)
        def _():
            load_tile(i + 1, other).start()

    h = h_ref[...]
    a = _swiglu_act(_dot(h, wg_ref[...]), _dot(h, wu_ref[...]))
    xo[slot] += _dot(a.astype(BF16), wo_ref[...].astype(BF16))

    @pl.when(j == last_j)
    def _():
        if norm_output:
            xo[slot] = _rms_normalize(xo[slot], out_gain_ref[...])
        store_tile(i, slot).start()

        @pl.when(i == n_row_tiles - 1)
        def _():
            store_tile(i, slot).wait()


def _ffn(x, gain, w_gu, w_gu_tail, w_down, out_gain, layer, which, *, norm_output=False, tm=1024):
    t, d = x.shape
    tf = FFN_TILE
    n_col_tiles = (D_FF - FFN_TAIL) // tf
    assert t % tm == 0 and n_col_tiles >= 2
    tail_block = (D_FF - FFN_TAIL) // FFN_TAIL
    return pl.pallas_call(
        functools.partial(_ffn_kernel, norm_output),
        grid=(t // tm, n_col_tiles),
        in_specs=[
            pl.BlockSpec(memory_space=pl.ANY),
            pl.BlockSpec((None, None, 1, d), lambda i, j: (layer, which, 0, 0)),
            pl.BlockSpec((None, None, None, d, tf), lambda i, j: (layer, which, 0, 0, j)),
            pl.BlockSpec((None, None, None, d, tf), lambda i, j: (layer, which, 1, 0, j)),
            pl.BlockSpec((None, None, tf, d), lambda i, j: (layer, which, j, 0)),
            pl.BlockSpec((None, None, d, 2 * FFN_TAIL), lambda i, j: (layer, which, 0, 0)),
            pl.BlockSpec((None, None, FFN_TAIL, d), lambda i, j: (layer, which, tail_block, 0)),
            pl.BlockSpec((1, d), lambda i, j: (0, 0)),
        ],
        out_specs=pl.BlockSpec(memory_space=pl.ANY),
        out_shape=jax.ShapeDtypeStruct((t, d), F32),
        scratch_shapes=[pltpu.VMEM((2, tm, d), F32), pltpu.VMEM((tm, d), BF16),
                        pltpu.SemaphoreType.DMA((2,)), pltpu.SemaphoreType.DMA((2,))],
        compiler_params=_params(("arbitrary", "arbitrary"), 58),
        name="ffn",
    )(x, gain, w_gu, w_gu, w_down, w_gu_tail, w_down, out_gain)


def _norm_matmul_kernel(x_ref, gain_ref, w_ref, o_ref, h_ref):
    @pl.when(pl.program_id(1) == 0)
    def _():
        h_ref[...] = _rms_normalize(x_ref[...], gain_ref[...]).astype(BF16)

    o_ref[...] = _dot(h_ref[...], w_ref[...]).astype(o_ref.dtype)


def _norm_matmul(x, gain, w, layer, *, tm=1024, tn=2560):
    t, d = x.shape
    n_cols = w.shape[-1]
    return pl.pallas_call(
        _norm_matmul_kernel,
        grid=(t // tm, n_cols // tn),
        in_specs=[
            pl.BlockSpec((tm, d), lambda i, j: (i, 0)),
            pl.BlockSpec((None, 1, d), lambda i, j: (layer, 0, 0)),
            pl.BlockSpec((None, d, tn), lambda i, j: (layer, 0, j)),
        ],
        out_specs=pl.BlockSpec((tm, tn), lambda i, j: (i, j)),
        out_shape=jax.ShapeDtypeStruct((t, n_cols), BF16),
        scratch_shapes=[pltpu.VMEM((tm, d), BF16)],
        compiler_params=_params(("parallel", "arbitrary"), 58),
        name="in_proj",
    )(x, gain, w)


def _attn_kernel(q0, k0, v0, q1, k1, v1, q2, k2, v2, o_ref,
                 qs1, ks1, vs1, qs2, ks2, vs2, og0, og1, og2, lg0, lg1, lg2):
    nb = ATTN_BLOCK
    row2 = lax.broadcasted_iota(jnp.int32, (nb, 2 * nb), 0)
    col2 = lax.broadcasted_iota(jnp.int32, (nb, 2 * nb), 1)
    band = (col2 >= row2) & (col2 <= row2 + nb)
    row1 = lax.broadcasted_iota(jnp.int32, (nb, nb), 0)
    col1 = lax.broadcasted_iota(jnp.int32, (nb, nb), 1)
    causal = col1 <= row1

    ones = jnp.ones((2 * nb, HEAD_DIM), BF16)

    def block(q, k, v, mask):
        s = jnp.where(mask, _dot_nt(q, k), MASK_VALUE)
        m = jnp.max(s, axis=-1, keepdims=True)
        p = jnp.exp(s - m).astype(BF16)
        ov = _dot(p, jnp.concatenate([v, ones[:v.shape[0]]], axis=1))
        o, l = ov[:, :HEAD_DIM], ov[:, HEAD_DIM:]
        return o / l, m + jnp.log(l)

    def run_group(dilation, q_ref, k_ref, v_ref, og, lg, qs=None, ks=None, vs=None):
        r = dilation
        nblk = SEQ // (r * nb)
        if r > 1:
            qs[...] = q_ref[...].astype(F32)
            ks[...] = k_ref[...].astype(F32)
            vs[...] = v_ref[...].astype(F32)
        for c in range(r):
            for n in range(nblk):
                first = n == 0
                nk = nb if first else 2 * nb
                q_start = c + r * nb * n
                k_start = q_start if first else q_start - r * nb
                if r == 1:
                    q = q_ref[pl.ds(q_start, nb), :]
                    k = k_ref[pl.ds(k_start, nk), :]
                    v = v_ref[pl.ds(k_start, nk), :]
                    rows = pl.ds(q_start, nb)
                else:
                    q = qs[pl.ds(q_start, nb, stride=r), :].astype(BF16)
                    k = ks[pl.ds(k_start, nk, stride=r), :].astype(BF16)
                    v = vs[pl.ds(k_start, nk, stride=r), :].astype(BF16)
                    rows = pl.ds(q_start, nb, stride=r)
                o, lse = block(q, k, v, causal if first else band)
                og[rows, :] = o
                lg[rows, :] = lse

    run_group(DILATED_GROUPS[0][1], q0, k0, v0, og0, lg0)
    run_group(DILATED_GROUPS[1][1], q1, k1, v1, og1, lg1, qs1, ks1, vs1)
    run_group(DILATED_GROUPS[2][1], q2, k2, v2, og2, lg2, qs2, ks2, vs2)

    tile = 256
    for t in range(SEQ // tile):
        rows = pl.ds(t * tile, tile)
        l0, l1, l2 = lg0[rows, :], lg1[rows, :], lg2[rows, :]
        mx = jnp.maximum(jnp.maximum(l0, l1), l2)
        w0, w1, w2 = jnp.exp(l0 - mx), jnp.exp(l1 - mx), jnp.exp(l2 - mx)
        num = w0 * og0[rows, :] + w1 * og1[rows, :] + w2 * og2[rows, :]
        o_ref[rows, :] = (num / (w0 + w1 + w2)).astype(BF16)


def _attention(z):
    b = z.shape[0]
    heads_per_part = ATTN_WIDTH // HEAD_DIM

    def spec(part, group):
        base = Z_ATTN // HEAD_DIM + part * heads_per_part + group * ATTN_HEADS
        return pl.BlockSpec((None, SEQ, HEAD_DIM), lambda bi, h: (bi, 0, base + h))

    in_specs = [spec(part, group) for group in range(3) for part in range(3)]
    seq_f32 = pltpu.VMEM((SEQ, HEAD_DIM), F32)
    return pl.pallas_call(
        _attn_kernel,
        grid=(b, ATTN_HEADS),
        in_specs=in_specs,
        out_specs=pl.BlockSpec((None, SEQ, HEAD_DIM), lambda bi, h: (bi, 0, h)),
        out_shape=jax.ShapeDtypeStruct((b, SEQ, GROUP_WIDTH), BF16),
        scratch_shapes=[seq_f32] * 12,
        compiler_params=_params(("parallel", "parallel"), 40),
        name="dilated_attn",
    )(*([z] * 9))


def _hgrn_masks():
    c = HGRN_CHUNK
    t = np.arange(c)[:, None]
    s = np.arange(c)[None, :]
    masks = []
    for h in HGRN_LEVELS:
        masks.append((t // (2 * h) == s // (2 * h)) & (t % (2 * h) >= h) & (s % (2 * h) < h))
    masks.append(t == s)
    return np.stack(masks).astype(np.float32)


def _hgrn_kernel(layer, zq, zf, zi, zg, lb_ref, gn_ref, mk_ref, o_ref, st_ref, b_sc):
    c = HGRN_CHUNK
    tc = zq.shape[0]

    @pl.when(pl.program_id(1) == 0)
    def _():
        st_ref[...] = jnp.zeros_like(st_ref)

    lb_all = lb_ref[...]
    e = jnp.exp(lb_all - jnp.max(lb_all, axis=0, keepdims=True))
    p = e / jnp.sum(e, axis=0, keepdims=True)
    lower = jnp.zeros((1, HGRN_WIDTH), F32)
    for i in range(1, layer + 1):
        lower = lower + p[i:i + 1, :]
    gain = gn_ref[layer:layer + 1, :]

    rowi = lax.broadcasted_iota(jnp.int32, (c, HGRN_DIM), 0)
    sub8 = rowi % 8
    odd2 = rowi % 2 == 1
    r4 = rowi % 4
    sign_bit = jnp.uint32(0x80000000)
    head_cols = [slice(hd * HGRN_DIM, (hd + 1) * HGRN_DIM) for hd in range(HGRN_HEADS)]
    head_lb = [lower[:, cols] for cols in head_cols]
    head_oml = [1.0 - lb for lb in head_lb]

    def roll_in_vreg(v, d):
        return jnp.concatenate([pltpu.roll(v[r:r + 8], d, axis=0) for r in range(0, c, 8)], axis=0)

    def chunk(ci, carry):
        rows = pl.ds(pl.multiple_of(ci * c, c), c)
        for hd in range(HGRN_HEADS):
            cols = head_cols[hd]
            lb, oml = head_lb[hd], head_oml[hd]
            q = zq[rows, cols].astype(F32)
            fl = zf[rows, cols].astype(F32)
            iv = zi[rows, cols]
            ef = jnp.exp(-fl)
            sg = 1.0 / (1.0 + ef)
            lf = jnp.log(lb + oml * sg)
            kk = oml * (ef * sg)
            qf = q / (1.0 + jnp.exp(-q))

            b = lf
            for d in (1, 2, 4):
                b = b + jnp.where(sub8 >= d, roll_in_vreg(b, d), 0.0)
            groups = [b[0:8]]
            for r in range(8, c, 8):
                groups.append(b[r:r + 8] + jnp.broadcast_to(groups[-1][7:8], (8, HGRN_DIM)))
            b = jnp.concatenate(groups, axis=0)
            b_sc[hd] = b
            b_last = b_sc[hd, c - 1:c, :]

            qb = qf.astype(BF16)
            kb = kk.astype(BF16)
            scores = mk_ref[len(HGRN_LEVELS)] * jnp.sum(qf * kk, axis=-1, keepdims=True)
            for lev, h in enumerate(HGRN_LEVELS):
                if h == 1:
                    g = jnp.where(odd2, roll_in_vreg(b, 1), b)
                elif h == 2:
                    g = jnp.where(r4 == 0, roll_in_vreg(b, 7),
                                  jnp.where(r4 == 1, b,
                                            jnp.where(r4 == 2, roll_in_vreg(b, 1), roll_in_vreg(b, 2))))
                else:
                    g = jnp.concatenate(
                        [jnp.broadcast_to(b_sc[hd, pl.ds(p0 + h - 1, 1), :], (2 * h, HGRN_DIM))
                         for p0 in range(0, c, 2 * h)], axis=0)
                neg_abs = lax.bitcast_convert_type(
                    lax.bitcast_convert_type(b - g, jnp.uint32) | sign_bit, F32)
                ed = jnp.exp(neg_abs).astype(BF16)
                scores = scores + _dot_nt(qb * ed, kb * ed) * mk_ref[lev]

            st = st_ref[hd]
            o = _dot_nt((qf * jnp.exp(b)).astype(BF16), st.astype(BF16))
            o = o + _dot(scores.astype(BF16), iv)
            kd = (kk * jnp.exp(b_last - b)).astype(BF16)
            st_ref[hd] = jnp.exp(b_last) * st + _dot_tn(iv, kd)

            gt = zg[rows, cols].astype(F32)
            y = _rms_normalize(o, gain) * (gt / (1.0 + jnp.exp(-gt)))
            o_ref[rows, cols] = y.astype(BF16)
        return carry

    lax.fori_loop(0, tc // c, chunk, 0)


def _hgrn(z, hgrn_lb, hgrn_norm, masks, layer, *, tc=512):
    b = z.shape[0]

    def spec(part):
        return pl.BlockSpec((None, tc, HGRN_WIDTH), lambda bi, t: (bi, t, part))

    return pl.pallas_call(
        functools.partial(_hgrn_kernel, layer),
        grid=(b, SEQ // tc),
        in_specs=[spec(0), spec(1), spec(2), spec(3),
                  pl.BlockSpec(hgrn_lb.shape, lambda bi, t: (0, 0)),
                  pl.BlockSpec(hgrn_norm.shape, lambda bi, t: (0, 0)),
                  pl.BlockSpec(masks.shape, lambda bi, t: (0, 0, 0))],
        out_specs=pl.BlockSpec((None, tc, HGRN_WIDTH), lambda bi, t: (bi, t, 0)),
        out_shape=jax.ShapeDtypeStruct((b, SEQ, HGRN_WIDTH), BF16),
        scratch_shapes=[pltpu.VMEM((HGRN_HEADS, HGRN_DIM, HGRN_DIM), F32),
                        pltpu.VMEM((HGRN_HEADS, HGRN_CHUNK, HGRN_DIM), F32)],
        compiler_params=_params(("parallel", "arbitrary"), 40),
        name="hgrn2",
    )(z, z, z, z, hgrn_lb, hgrn_norm, masks)


def _combine_kernel(x_ref, at_ref, hg_ref, gl_ref, bg_ref, wpa_ref, wpm_ref, wo_ref, o_ref, y_ref):
    d = o_ref.shape[1]
    tn = 512
    at = at_ref[...]
    hg = hg_ref[...]
    for c0 in range(0, d, tn):
        ga = jax.nn.sigmoid(gl_ref[:, c0:c0 + tn].astype(F32) + bg_ref[:, c0:c0 + tn])
        gm = jax.nn.sigmoid(gl_ref[:, d + c0:d + c0 + tn].astype(F32) + bg_ref[:, d + c0:d + c0 + tn])
        y = ga * _dot(at, wpa_ref[:, c0:c0 + tn]) + gm * _dot(hg, wpm_ref[:, c0:c0 + tn])
        y_ref[:, c0:c0 + tn] = y.astype(BF16)
    o_ref[...] = x_ref[...] + _dot(y_ref[...], wo_ref[...])


def _combine(x, attn, hgrn, z, b_gate, w_pa, w_pm, w_o, layer, *, tm=512):
    t, d = x.shape

    def rows(width, col_block=0):
        return pl.BlockSpec((tm, width), lambda i: (i, col_block))

    def whole(a):
        return pl.BlockSpec((None,) + a.shape[1:], lambda i: (layer,) + (0,) * (a.ndim - 1),
                            pipeline_mode=pl.Buffered(1))

    return pl.pallas_call(
        _combine_kernel,
        grid=(t // tm,),
        in_specs=[rows(d), rows(GROUP_WIDTH), rows(HGRN_WIDTH), rows(2 * d, Z_GATE // (2 * d)),
                  whole(b_gate), whole(w_pa), whole(w_pm), whole(w_o)],
        out_specs=rows(d),
        out_shape=jax.ShapeDtypeStruct((t, d), F32),
        scratch_shapes=[pltpu.VMEM((tm, d), BF16)],
        compiler_params=_params(("parallel",), 56),
        name="merge_out_proj",
    )(x, attn, hgrn, z, b_gate, w_pa, w_pm, w_o)


def kernel(x, ffn_norm, ffn_w_in, ffn_w_out, mix_norm, w_in, b_gate, hgrn_lb, hgrn_norm,
           w_proj_attn, w_proj_hgrn, w_out, final_norm):
    bsz, seq, d = x.shape
    assert seq == SEQ and d == D_MODEL
    t = bsz * seq

    w_gu, w_gu_tail = _split_ffn_in(ffn_w_in)
    w_down = ffn_w_out
    w_mix = _reorder_mix_in(w_in)
    w_pa = w_proj_attn.astype(BF16)
    w_pm = w_proj_hgrn.astype(BF16)
    w_o = w_out.astype(BF16)
    ffn_gain = ffn_norm.reshape(DEPTH, 2, 1, d)
    mix_gain = mix_norm.reshape(DEPTH, 1, d)
    gate_bias = b_gate.reshape(DEPTH, 1, 2 * d)
    masks = jnp.asarray(_hgrn_masks())

    out_gain = final_norm.reshape(1, d)
    xf = x.reshape(t, d)
    for l in range(DEPTH):
        xf = _ffn(xf, ffn_gain, w_gu, w_gu_tail, w_down, out_gain, l, 0)
        z = _norm_matmul(xf, mix_gain, w_mix, l)
        z3 = z.reshape(bsz, seq, Z_WIDTH)
        attn = _attention(z3)
        hgrn = _hgrn(z3, hgrn_lb, hgrn_norm, masks, l)
        xf = _combine(xf, attn.reshape(t, GROUP_WIDTH), hgrn.reshape(t, HGRN_WIDTH), z,
                      gate_bias, w_pa, w_pm, w_o, l)
        xf = _ffn(xf, ffn_gain, w_gu, w_gu_tail, w_down, out_gain, l, 1, norm_output=l == DEPTH - 1)
    return xf.reshape(bsz, seq, d)
```

```python
import functools

import numpy as np
import jax
import jax.numpy as jnp
from jax import lax
from jax.experimental import pallas as pl
from jax.experimental.pallas import tpu as pltpu

F32 = jnp.float32
BF16 = jnp.bfloat16

D_MODEL = 2048
SEQ = 2048
DEPTH = 4
HEAD_DIM = 128
ATTN_HEADS = 4
DILATED_GROUPS = ((128, 1), (512, 4), (2048, 16))
GROUP_WIDTH = ATTN_HEADS * HEAD_DIM
ATTN_WIDTH = len(DILATED_GROUPS) * GROUP_WIDTH
HGRN_HEADS = 8
HGRN_DIM = 128
HGRN_WIDTH = HGRN_HEADS * HGRN_DIM
Z_GATE = 4 * HGRN_WIDTH
Z_ATTN = Z_GATE + 2 * D_MODEL
Z_WIDTH = Z_ATTN + 3 * ATTN_WIDTH
D_FF = 5504
FFN_TILE = 768
FFN_TAIL = 128
MACARON_WEIGHT = 0.5
EPS = 1e-6
MASK_VALUE = -1e30
LANES = 128
ATTN_BLOCK = 128
HGRN_CHUNK = 128
HGRN_LEVELS = (1, 2, 4, 8, 16, 32, 64)
MIB = 1024 * 1024


def _rms_normalize(x, gain):
    ms = jnp.mean(x * x, axis=-1, keepdims=True)
    return x * lax.rsqrt(ms + EPS) * gain


def _dot(a, b):
    return jnp.dot(a, b, preferred_element_type=F32)


def _dot_nt(a, b):
    return lax.dot_general(a, b, (((1,), (1,)), ((), ())), preferred_element_type=F32)


def _dot_tn(a, b):
    return lax.dot_general(a, b, (((0,), (0,)), ((), ())), preferred_element_type=F32)


def _params(semantics, vmem_mib):
    return pltpu.CompilerParams(dimension_semantics=semantics, vmem_limit_bytes=vmem_mib * MIB)


def _split_ffn_in_kernel(src_ref, dst_ref, tail_ref):
    v = src_ref[...].astype(BF16)
    dst_ref[...] = v
    tail_ref[...] = v[:, D_FF - FFN_TAIL:]


def _split_ffn_in(ffn_w_in, *, tr=256):
    depth, two, d, _ = ffn_w_in.shape
    return pl.pallas_call(
        _split_ffn_in_kernel,
        grid=(depth, two, 2, d // tr),
        in_specs=[pl.BlockSpec((None, None, tr, D_FF), lambda l, w, k, r: (l, w, r, k))],
        out_specs=[pl.BlockSpec((None, None, None, tr, D_FF), lambda l, w, k, r: (l, w, k, r, 0)),
                   pl.BlockSpec((None, None, tr, FFN_TAIL), lambda l, w, k, r: (l, w, r, k))],
        out_shape=[jax.ShapeDtypeStruct((depth, two, 2, d, D_FF), BF16),
                   jax.ShapeDtypeStruct((depth, two, d, 2 * FFN_TAIL), BF16)],
        compiler_params=_params(("parallel",) * 4, 32),
        name="split_ffn_in",
    )(ffn_w_in)


def _reorder_mix_in_kernel(n_query_blocks, src_ref, dst_ref):
    c = jnp.where(pl.program_id(1) < n_query_blocks, HEAD_DIM ** -0.5, 1.0)
    dst_ref[...] = (src_ref[...] * c).astype(BF16)


def _reorder_mix_in(w_in, *, tn=512):
    depth, d, n = w_in.shape
    nblk = n // tn
    shift = Z_ATTN // tn
    assert ATTN_WIDTH % tn == 0
    return pl.pallas_call(
        functools.partial(_reorder_mix_in_kernel, ATTN_WIDTH // tn),
        grid=(depth, nblk),
        in_specs=[pl.BlockSpec((None, d, tn), lambda l, j: (l, 0, j))],
        out_specs=pl.BlockSpec((None, d, tn), lambda l, j: (l, 0, lax.rem(j + shift, nblk))),
        out_shape=jax.ShapeDtypeStruct((depth, d, n), BF16),
        compiler_params=_params(("parallel", "parallel"), 32),
        name="reorder_mix_in",
    )(w_in)


def _swiglu_act(g, u):
    return (MACARON_WEIGHT * g * jax.nn.sigmoid(g)) * u


def _ffn_kernel(norm_output, x_hbm, gain_ref, wg_ref, wu_ref, wo_ref, wgu_tail_ref, wo_tail_ref,
                out_gain_ref, o_hbm, xo, h_ref, sem_in, sem_out):
    i = pl.program_id(0)
    j = pl.program_id(1)
    n_row_tiles = pl.num_programs(0)
    last_j = pl.num_programs(1) - 1
    tm = xo.shape[1]
    slot = lax.rem(i, 2)
    other = 1 - slot

    def load_tile(tile, s):
        return pltpu.make_async_copy(x_hbm.at[pl.ds(tile * tm, tm), :], xo.at[s], sem_in.at[s])

    def store_tile(tile, s):
        return pltpu.make_async_copy(xo.at[s], o_hbm.at[pl.ds(tile * tm, tm), :], sem_out.at[s])

    @pl.when(j == 0)
    def _():
        @pl.when(i == 0)
        def _():
            load_tile(0, 0).start()

        load_tile(i, slot).wait()
        h_ref[...] = _rms_normalize(xo[slot], gain_ref[...]).astype(BF16)
        gu = jnp.concatenate([_dot(h_ref[r:r + tm // 2, :], wgu_tail_ref[...])
                              for r in (0, tm // 2)], axis=0)
        a = _swiglu_act(gu[:, :FFN_TAIL], gu[:, FFN_TAIL:]).astype(BF16)
        half = xo.shape[2] // 2
        for c0 in (0, half):
            xo[slot, :, c0:c0 + half] += _dot(a, wo_tail_ref[:, c0:c0 + half].astype(BF16))

    @pl.when(j == 1)
    def _():
        @pl.when(i > 0)
        def _():
            store_tile(i - 1, other).wait()

        @pl.when(i + 1 < n_row_tiles)
        def _():
            load_tile(i + 1, other).start()

    h = h_ref[...]
    a = _swiglu_act(_dot(h, wg_ref[...]), _dot(h, wu_ref[...]))
    xo[slot] += _dot(a.astype(BF16), wo_ref[...].astype(BF16))

    @pl.when(j == last_j)
    def _():
        if norm_output:
            xo[slot] = _rms_normalize(xo[slot], out_gain_ref[...])
        store_tile(i, slot).start()

        @pl.when(i == n_row_tiles - 1)
        def _():
            store_tile(i, slot).wait()


def _ffn(x, gain, w_gu, w_gu_tail, w_down, out_gain, layer, which, *, norm_output=False, tm=1024):
    t, d = x.shape
    tf = FFN_TILE
    n_col_tiles = (D_FF - FFN_TAIL) // tf
    assert t % tm == 0 and n_col_tiles >= 2
    tail_block = (D_FF - FFN_TAIL) // FFN_TAIL
    return pl.pallas_call(
        functools.partial(_ffn_kernel, norm_output),
        grid=(t // tm, n_col_tiles),
        in_specs=[
            pl.BlockSpec(memory_space=pl.ANY),
            pl.BlockSpec((None, None, 1, d), lambda i, j: (layer, which, 0, 0)),
            pl.BlockSpec((None, None, None, d, tf), lambda i, j: (layer, which, 0, 0, j)),
            pl.BlockSpec((None, None, None, d, tf), lambda i, j: (layer, which, 1, 0, j)),
            pl.BlockSpec((None, None, tf, d), lambda i, j: (layer, which, j, 0)),
            pl.BlockSpec((None, None, d, 2 * FFN_TAIL), lambda i, j: (layer, which, 0, 0)),
            pl.BlockSpec((None, None, FFN_TAIL, d), lambda i, j: (layer, which, tail_block, 0)),
            pl.BlockSpec((1, d), lambda i, j: (0, 0)),
        ],
        out_specs=pl.BlockSpec(memory_space=pl.ANY),
        out_shape=jax.ShapeDtypeStruct((t, d), F32),
        scratch_shapes=[pltpu.VMEM((2, tm, d), F32), pltpu.VMEM((tm, d), BF16),
                        pltpu.SemaphoreType.DMA((2,)), pltpu.SemaphoreType.DMA((2,))],
        compiler_params=_params(("arbitrary", "arbitrary"), 58),
        name="ffn",
    )(x, gain, w_gu, w_gu, w_down, w_gu_tail, w_down, out_gain)


def _norm_matmul_kernel(x_ref, gain_ref, w_ref, o_ref, h_ref):
    @pl.when(pl.program_id(1) == 0)
    def _():
        h_ref[...] = _rms_normalize(x_ref[...], gain_ref[...]).astype(BF16)

    o_ref[...] = _dot(h_ref[...], w_ref[...]).astype(o_ref.dtype)


def _norm_matmul(x, gain, w, layer, *, tm=1024, tn=2560):
    t, d = x.shape
    n_cols = w.shape[-1]
    return pl.pallas_call(
        _norm_matmul_kernel,
        grid=(t // tm, n_cols // tn),
        in_specs=[
            pl.BlockSpec((tm, d), lambda i, j: (i, 0)),
            pl.BlockSpec((None, 1, d), lambda i, j: (layer, 0, 0)),
            pl.BlockSpec((None, d, tn), lambda i, j: (layer, 0, j)),
        ],
        out_specs=pl.BlockSpec((tm, tn), lambda i, j: (i, j)),
        out_shape=jax.ShapeDtypeStruct((t, n_cols), BF16),
        scratch_shapes=[pltpu.VMEM((tm, d), BF16)],
        compiler_params=_params(("parallel", "arbitrary"), 58),
        name="in_proj",
    )(x, gain, w)


def _attn_kernel(q0, k0, v0, q1, k1, v1, q2, k2, v2, o_ref,
                 qs1, ks1, vs1, qs2, ks2, vs2, og0, og1, og2, lg0, lg1, lg2):
    nb = ATTN_BLOCK
    row2 = lax.broadcasted_iota(jnp.int32, (nb, 2 * nb), 0)
    col2 = lax.broadcasted_iota(jnp.int32, (nb, 2 * nb), 1)
    band = (col2 >= row2) & (col2 <= row2 + nb)
    row1 = lax.broadcasted_iota(jnp.int32, (nb, nb), 0)
    col1 = lax.broadcasted_iota(jnp.int32, (nb, nb), 1)
    causal = col1 <= row1

    ones = jnp.ones((2 * nb, HEAD_DIM), BF16)

    def block(q, k, v, mask):
        s = jnp.where(mask, _dot_nt(q, k), MASK_VALUE)
        m = jnp.max(s, axis=-1, keepdims=True)
        p = jnp.exp(s - m).astype(BF16)
        ov = _dot(p, jnp.concatenate([v, ones[:v.shape[0]]], axis=1))
        o, l = ov[:, :HEAD_DIM], ov[:, HEAD_DIM:]
        return o / l, m + jnp.log(l)

    def run_group(dilation, q_ref, k_ref, v_ref, og, lg, qs=None, ks=None, vs=None):
        r = dilation
        nblk = SEQ // (r * nb)
        if r > 1:
            qs[...] = q_ref[...].astype(F32)
            ks[...] = k_ref[...].astype(F32)
            vs[...] = v_ref[...].astype(F32)
        for c in range(r):
            for n in range(nblk):
                first = n == 0
                nk = nb if first else 2 * nb
                q_start = c + r * nb * n
                k_start = q_start if first else q_start - r * nb
                if r == 1:
                    q = q_ref[pl.ds(q_start, nb), :]
                    k = k_ref[pl.ds(k_start, nk), :]
                    v = v_ref[pl.ds(k_start, nk), :]
                    rows = pl.ds(q_start, nb)
                else:
                    q = qs[pl.ds(q_start, nb, stride=r), :].astype(BF16)
                    k = ks[pl.ds(k_start, nk, stride=r), :].astype(BF16)
                    v = vs[pl.ds(k_start, nk, stride=r), :].astype(BF16)
                    rows = pl.ds(q_start, nb, stride=r)
                o, lse = block(q, k, v, causal if first else band)
                og[rows, :] = o
                lg[rows, :] = lse

    run_group(DILATED_GROUPS[0][1], q0, k0, v0, og0, lg0)
    run_group(DILATED_GROUPS[1][1], q1, k1, v1, og1, lg1, qs1, ks1, vs1)
    run_group(DILATED_GROUPS[2][1], q2, k2, v2, og2, lg2, qs2, ks2, vs2)

    tile = 256
    for t in range(SEQ // tile):
        rows = pl.ds(t * tile, tile)
        l0, l1, l2 = lg0[rows, :], lg1[rows, :], lg2[rows, :]
        mx = jnp.maximum(jnp.maximum(l0, l1), l2)
        w0, w1, w2 = jnp.exp(l0 - mx), jnp.exp(l1 - mx), jnp.exp(l2 - mx)
        num = w0 * og0[rows, :] + w1 * og1[rows, :] + w2 * og2[rows, :]
        o_ref[rows, :] = (num / (w0 + w1 + w2)).astype(BF16)


def _attention(z):
    b = z.shape[0]
    heads_per_part = ATTN_WIDTH // HEAD_DIM

    def spec(part, group):
        base = Z_ATTN // HEAD_DIM + part * heads_per_part + group * ATTN_HEADS
        return pl.BlockSpec((None, SEQ, HEAD_DIM), lambda bi, h: (bi, 0, base + h))

    in_specs = [spec(part, group) for group in range(3) for part in range(3)]
    seq_f32 = pltpu.VMEM((SEQ, HEAD_DIM), F32)
    return pl.pallas_call(
        _attn_kernel,
        grid=(b, ATTN_HEADS),
        in_specs=in_specs,
        out_specs=pl.BlockSpec((None, SEQ, HEAD_DIM), lambda bi, h: (bi, 0, h)),
        out_shape=jax.ShapeDtypeStruct((b, SEQ, GROUP_WIDTH), BF16),
        scratch_shapes=[seq_f32] * 12,
        compiler_params=_params(("parallel", "parallel"), 40),
        name="dilated_attn",
    )(*([z] * 9))


def _hgrn_masks():
    c = HGRN_CHUNK
    t = np.arange(c)[:, None]
    s = np.arange(c)[None, :]
    masks = []
    for h in HGRN_LEVELS:
        masks.append((t // (2 * h) == s // (2 * h)) & (t % (2 * h) >= h) & (s % (2 * h) < h))
    masks.append(t == s)
    return np.stack(masks).astype(np.float32)


def _hgrn_kernel(layer, zq, zf, zi, zg, lb_ref, gn_ref, mk_ref, o_ref, st_ref, b_sc):
    c = HGRN_CHUNK
    tc = zq.shape[0]

    @pl.when(pl.program_id(1) == 0)
    def _():
        st_ref[...] = jnp.zeros_like(st_ref)

    lb_all = lb_ref[...]
    e = jnp.exp(lb_all - jnp.max(lb_all, axis=0, keepdims=True))
    p = e / jnp.sum(e, axis=0, keepdims=True)
    lower = jnp.zeros((1, HGRN_WIDTH), F32)
    for i in range(1, layer + 1):
        lower = lower + p[i:i + 1, :]
    gain = gn_ref[layer:layer + 1, :]

    rowi = lax.broadcasted_iota(jnp.int32, (c, HGRN_DIM), 0)
    sub8 = rowi % 8
    odd2 = rowi % 2 == 1
    r4 = rowi % 4
    sign_bit = jnp.uint32(0x80000000)
    head_cols = [slice(hd * HGRN_DIM, (hd + 1) * HGRN_DIM) for hd in range(HGRN_HEADS)]
    head_lb = [lower[:, cols] for cols in head_cols]
    head_oml = [1.0 - lb for lb in head_lb]

    def roll_in_vreg(v, d):
        return jnp.concatenate([pltpu.roll(v[r:r + 8], d, axis=0) for r in range(0, c, 8)], axis=0)

    def chunk(ci, carry):
        rows = pl.ds(pl.multiple_of(ci * c, c), c)
        for hd in range(HGRN_HEADS):
            cols = head_cols[hd]
            lb, oml = head_lb[hd], head_oml[hd]
            q = zq[rows, cols].astype(F32)
            fl = zf[rows, cols].astype(F32)
            iv = zi[rows, cols]
            sg = 1.0 / (1.0 + jnp.exp(-fl))
            lf = jnp.log(lb + oml * sg)
            kk = oml * (1.0 - sg)
            qf = q / (1.0 + jnp.exp(-q))

            b = lf
            for d in (1, 2, 4):
                b = b + jnp.where(sub8 >= d, roll_in_vreg(b, d), 0.0)
            groups = [b[0:8]]
            for r in range(8, c, 8):
                groups.append(b[r:r + 8] + jnp.broadcast_to(groups[-1][7:8], (8, HGRN_DIM)))
            b = jnp.concatenate(groups, axis=0)
            b_sc[hd] = b
            b_last = b_sc[hd, c - 1:c, :]

            qb = qf.astype(BF16)
            kb = kk.astype(BF16)
            scores = mk_ref[len(HGRN_LEVELS)] * jnp.sum(qf * kk, axis=-1, keepdims=True)
            for lev, h in enumerate(HGRN_LEVELS):
                if h == 1:
                    g = jnp.where(odd2, roll_in_vreg(b, 1), b)
                elif h == 2:
                    g = jnp.where(r4 == 0, roll_in_vreg(b, 7),
                                  jnp.where(r4 == 1, b,
                                            jnp.where(r4 == 2, roll_in_vreg(b, 1), roll_in_vreg(b, 2))))
                else:
                    g = jnp.concatenate(
                        [jnp.broadcast_to(b_sc[hd, pl.ds(p0 + h - 1, 1), :], (2 * h, HGRN_DIM))
                         for p0 in range(0, c, 2 * h)], axis=0)
                neg_abs = lax.bitcast_convert_type(
                    lax.bitcast_convert_type(b - g, jnp.uint32) | sign_bit, F32)
                ed = jnp.exp(neg_abs).astype(BF16)
                scores = scores + _dot_nt(qb * ed, kb * ed) * mk_ref[lev]

            st = st_ref[hd]
            o = _dot_nt((qf * jnp.exp(b)).astype(BF16), st.astype(BF16))
            o = o + _dot(scores.astype(BF16), iv)
            kd = (kk * jnp.exp(b_last - b)).astype(BF16)
            st_ref[hd] = jnp.exp(b_last) * st + _dot_tn(iv, kd)

            gt = zg[rows, cols].astype(F32)
            y = _rms_normalize(o, gain) * (gt / (1.0 + jnp.exp(-gt)))
            o_ref[rows, cols] = y.astype(BF16)
        return carry

    lax.fori_loop(0, tc // c, chunk, 0)


def _hgrn(z, hgrn_lb, hgrn_norm, masks, layer, *, tc=512):
    b = z.shape[0]

    def spec(part):
        return pl.BlockSpec((None, tc, HGRN_WIDTH), lambda bi, t: (bi, t, part))

    return pl.pallas_call(
        functools.partial(_hgrn_kernel, layer),
        grid=(b, SEQ // tc),
        in_specs=[spec(0), spec(1), spec(2), spec(3),
                  pl.BlockSpec(hgrn_lb.shape, lambda bi, t: (0, 0)),
                  pl.BlockSpec(hgrn_norm.shape, lambda bi, t: (0, 0)),
                  pl.BlockSpec(masks.shape, lambda bi, t: (0, 0, 0))],
        out_specs=pl.BlockSpec((None, tc, HGRN_WIDTH), lambda bi, t: (bi, t, 0)),
        out_shape=jax.ShapeDtypeStruct((b, SEQ, HGRN_WIDTH), BF16),
        scratch_shapes=[pltpu.VMEM((HGRN_HEADS, HGRN_DIM, HGRN_DIM), F32),
                        pltpu.VMEM((HGRN_HEADS, HGRN_CHUNK, HGRN_DIM), F32)],
        compiler_params=_params(("parallel", "arbitrary"), 40),
        name="hgrn2",
    )(z, z, z, z, hgrn_lb, hgrn_norm, masks)


def _combine_kernel(x_ref, at_ref, hg_ref, gl_ref, bg_ref, wpa_ref, wpm_ref, wo_ref, o_ref, y_ref):
    d = o_ref.shape[1]
    tn = 512
    at = at_ref[...]
    hg = hg_ref[...]
    for c0 in range(0, d, tn):
        ga = jax.nn.sigmoid(gl_ref[:, c0:c0 + tn].astype(F32) + bg_ref[:, c0:c0 + tn])
        gm = jax.nn.sigmoid(gl_ref[:, d + c0:d + c0 + tn].astype(F32) + bg_ref[:, d + c0:d + c0 + tn])
        y = ga * _dot(at, wpa_ref[:, c0:c0 + tn]) + gm * _dot(hg, wpm_ref[:, c0:c0 + tn])
        y_ref[:, c0:c0 + tn] = y.astype(BF16)
    o_ref[...] = x_ref[...] + _dot(y_ref[...], wo_ref[...])


def _combine(x, attn, hgrn, z, b_gate, w_pa, w_pm, w_o, layer, *, tm=512):
    t, d = x.shape

    def rows(width, col_block=0):
        return pl.BlockSpec((tm, width), lambda i: (i, col_block))

    def whole(a):
        return pl.BlockSpec((None,) + a.shape[1:], lambda i: (layer,) + (0,) * (a.ndim - 1),
                            pipeline_mode=pl.Buffered(1))

    return pl.pallas_call(
        _combine_kernel,
        grid=(t // tm,),
        in_specs=[rows(d), rows(GROUP_WIDTH), rows(HGRN_WIDTH), rows(2 * d, Z_GATE // (2 * d)),
                  whole(b_gate), whole(w_pa), whole(w_pm), whole(w_o)],
        out_specs=rows(d),
        out_shape=jax.ShapeDtypeStruct((t, d), F32),
        scratch_shapes=[pltpu.VMEM((tm, d), BF16)],
        compiler_params=_params(("parallel",), 56),
        name="merge_out_proj",
    )(x, attn, hgrn, z, b_gate, w_pa, w_pm, w_o)


def kernel(x, ffn_norm, ffn_w_in, ffn_w_out, mix_norm, w_in, b_gate, hgrn_lb, hgrn_norm,
           w_proj_attn, w_proj_hgrn, w_out, final_norm):
    bsz, seq, d = x.shape
    assert seq == SEQ and d == D_MODEL
    t = bsz * seq

    w_gu, w_gu_tail = _split_ffn_in(ffn_w_in)
    w_down = ffn_w_out
    w_mix = _reorder_mix_in(w_in)
    w_pa = w_proj_attn.astype(BF16)
    w_pm = w_proj_hgrn.astype(BF16)
    w_o = w_out.astype(BF16)
    ffn_gain = ffn_norm.reshape(DEPTH, 2, 1, d)
    mix_gain = mix_norm.reshape(DEPTH, 1, d)
    gate_bias = b_gate.reshape(DEPTH, 1, 2 * d)
    masks = jnp.asarray(_hgrn_masks())

    out_gain = final_norm.reshape(1, d)
    xf = x.reshape(t, d)
    for l in range(DEPTH):
        xf = _ffn(xf, ffn_gain, w_gu, w_gu_tail, w_down, out_gain, l, 0)
        z = _norm_matmul(xf, mix_gain, w_mix, l)
        z3 = z.reshape(bsz, seq, Z_WIDTH)
        attn = _attention(z3)
        hgrn = _hgrn(z3, hgrn_lb, hgrn_norm, masks, l)
        xf = _combine(xf, attn.reshape(t, GROUP_WIDTH), hgrn.reshape(t, HGRN_WIDTH), z,
                      gate_bias, w_pa, w_pm, w_o, l)
        xf = _ffn(xf, ffn_gain, w_gu, w_gu_tail, w_down, out_gain, l, 1, norm_output=l == DEPTH - 1)
    return xf.reshape(bsz, seq, d)
```

```python
import functools

import numpy as np
import jax
import jax.numpy as jnp
from jax import lax
from jax.experimental import pallas as pl
from jax.experimental.pallas import tpu as pltpu

F32 = jnp.float32
BF16 = jnp.bfloat16

D_MODEL = 2048
SEQ = 2048
DEPTH = 4
HEAD_DIM = 128
ATTN_HEADS = 4
DILATED_GROUPS = ((128, 1), (512, 4), (2048, 16))
GROUP_WIDTH = ATTN_HEADS * HEAD_DIM
ATTN_WIDTH = len(DILATED_GROUPS) * GROUP_WIDTH
HGRN_HEADS = 8
HGRN_DIM = 128
HGRN_WIDTH = HGRN_HEADS * HGRN_DIM
Z_GATE = 4 * HGRN_WIDTH
Z_ATTN = Z_GATE + 2 * D_MODEL
Z_WIDTH = Z_ATTN + 3 * ATTN_WIDTH
D_FF = 5504
FFN_TILE = 768
FFN_TAIL = 128
MACARON_WEIGHT = 0.5
EPS = 1e-6
MASK_VALUE = -1e30
LANES = 128
ATTN_BLOCK = 128
ATTN_INNER_STRIDE = 4
HGRN_CHUNK = 128
HGRN_LEVELS = (1, 2, 4, 8, 16, 32, 64)
MIB = 1024 * 1024


def _rms_normalize(x, gain):
    ms = jnp.mean(x * x, axis=-1, keepdims=True)
    return x * lax.rsqrt(ms + EPS) * gain


def _dot(a, b):
    return jnp.dot(a, b, preferred_element_type=F32)


def _dot_nt(a, b):
    return lax.dot_general(a, b, (((1,), (1,)), ((), ())), preferred_element_type=F32)


def _dot_tn(a, b):
    return lax.dot_general(a, b, (((0,), (0,)), ((), ())), preferred_element_type=F32)


def _params(semantics, vmem_mib):
    return pltpu.CompilerParams(dimension_semantics=semantics, vmem_limit_bytes=vmem_mib * MIB)


def _split_ffn_in_kernel(src_ref, dst_ref, tail_ref):
    v = src_ref[...].astype(BF16)
    dst_ref[...] = v
    tail_ref[...] = v[:, D_FF - FFN_TAIL:]


def _split_ffn_in(ffn_w_in, *, tr=256):
    depth, two, d, _ = ffn_w_in.shape
    return pl.pallas_call(
        _split_ffn_in_kernel,
        grid=(depth, two, 2, d // tr),
        in_specs=[pl.BlockSpec((None, None, tr, D_FF), lambda l, w, k, r: (l, w, r, k))],
        out_specs=[pl.BlockSpec((None, None, None, tr, D_FF), lambda l, w, k, r: (l, w, k, r, 0)),
                   pl.BlockSpec((None, None, tr, FFN_TAIL), lambda l, w, k, r: (l, w, r, k))],
        out_shape=[jax.ShapeDtypeStruct((depth, two, 2, d, D_FF), BF16),
                   jax.ShapeDtypeStruct((depth, two, d, 2 * FFN_TAIL), BF16)],
        compiler_params=_params(("parallel",) * 4, 32),
        name="split_ffn_in",
    )(ffn_w_in)


def _reorder_mix_in_kernel(n_query_blocks, src_ref, dst_ref):
    c = jnp.where(pl.program_id(1) < n_query_blocks, HEAD_DIM ** -0.5, 1.0)
    dst_ref[...] = (src_ref[...] * c).astype(BF16)


def _reorder_mix_in(w_in, *, tn=512):
    depth, d, n = w_in.shape
    nblk = n // tn
    shift = Z_ATTN // tn
    assert ATTN_WIDTH % tn == 0
    return pl.pallas_call(
        functools.partial(_reorder_mix_in_kernel, ATTN_WIDTH // tn),
        grid=(depth, nblk),
        in_specs=[pl.BlockSpec((None, d, tn), lambda l, j: (l, 0, j))],
        out_specs=pl.BlockSpec((None, d, tn), lambda l, j: (l, 0, lax.rem(j + shift, nblk))),
        out_shape=jax.ShapeDtypeStruct((depth, d, n), BF16),
        compiler_params=_params(("parallel", "parallel"), 32),
        name="reorder_mix_in",
    )(w_in)


def _swiglu_act(g, u):
    return (MACARON_WEIGHT * g * jax.nn.sigmoid(g)) * u


def _ffn_kernel(norm_output, x_hbm, gain_ref, wg_ref, wu_ref, wo_ref, wgu_tail_ref, wo_tail_ref,
                out_gain_ref, o_hbm, xo, h_ref, sem_in, sem_out):
    i = pl.program_id(0)
    j = pl.program_id(1)
    n_row_tiles = pl.num_programs(0)
    last_j = pl.num_programs(1) - 1
    tm = xo.shape[1]
    slot = lax.rem(i, 2)
    other = 1 - slot

    def load_tile(tile, s):
        return pltpu.make_async_copy(x_hbm.at[pl.ds(tile * tm, tm), :], xo.at[s], sem_in.at[s])

    def store_tile(tile, s):
        return pltpu.make_async_copy(xo.at[s], o_hbm.at[pl.ds(tile * tm, tm), :], sem_out.at[s])

    @pl.when(j == 0)
    def _():
        @pl.when(i == 0)
        def _():
            load_tile(0, 0).start()

        load_tile(i, slot).wait()
        h_ref[...] = _rms_normalize(xo[slot], gain_ref[...]).astype(BF16)
        gu = jnp.concatenate([_dot(h_ref[r:r + tm // 2, :], wgu_tail_ref[...])
                              for r in (0, tm // 2)], axis=0)
        a = _swiglu_act(gu[:, :FFN_TAIL], gu[:, FFN_TAIL:]).astype(BF16)
        half = xo.shape[2] // 2
        for c0 in (0, half):
            xo[slot, :, c0:c0 + half] += _dot(a, wo_tail_ref[:, c0:c0 + half].astype(BF16))

    @pl.when(j == 1)
    def _():
        @pl.when(i > 0)
        def _():
            store_tile(i - 1, other).wait()

        @pl.when(i + 1 < n_row_tiles)
        def _():
            load_tile(i + 1, other).start()

    h = h_ref[...]
    a = _swiglu_act(_dot(h, wg_ref[...]), _dot(h, wu_ref[...]))
    xo[slot] += _dot(a.astype(BF16), wo_ref[...].astype(BF16))

    @pl.when(j == last_j)
    def _():
        if norm_output:
            xo[slot] = _rms_normalize(xo[slot], out_gain_ref[...])
        store_tile(i, slot).start()

        @pl.when(i == n_row_tiles - 1)
        def _():
            store_tile(i, slot).wait()


def _ffn(x, gain, w_gu, w_gu_tail, w_down, out_gain, layer, which, *, norm_output=False, tm=1024):
    t, d = x.shape
    tf = FFN_TILE
    n_col_tiles = (D_FF - FFN_TAIL) // tf
    assert t % tm == 0 and n_col_tiles >= 2
    tail_block = (D_FF - FFN_TAIL) // FFN_TAIL
    return pl.pallas_call(
        functools.partial(_ffn_kernel, norm_output),
        grid=(t // tm, n_col_tiles),
        in_specs=[
            pl.BlockSpec(memory_space=pl.ANY),
            pl.BlockSpec((None, None, 1, d), lambda i, j: (layer, which, 0, 0)),
            pl.BlockSpec((None, None, None, d, tf), lambda i, j: (layer, which, 0, 0, j)),
            pl.BlockSpec((None, None, None, d, tf), lambda i, j: (layer, which, 1, 0, j)),
            pl.BlockSpec((None, None, tf, d), lambda i, j: (layer, which, j, 0)),
            pl.BlockSpec((None, None, d, 2 * FFN_TAIL), lambda i, j: (layer, which, 0, 0)),
            pl.BlockSpec((None, None, FFN_TAIL, d), lambda i, j: (layer, which, tail_block, 0)),
            pl.BlockSpec((1, d), lambda i, j: (0, 0)),
        ],
        out_specs=pl.BlockSpec(memory_space=pl.ANY),
        out_shape=jax.ShapeDtypeStruct((t, d), F32),
        scratch_shapes=[pltpu.VMEM((2, tm, d), F32), pltpu.VMEM((tm, d), BF16),
                        pltpu.SemaphoreType.DMA((2,)), pltpu.SemaphoreType.DMA((2,))],
        compiler_params=_params(("arbitrary", "arbitrary"), 58),
        name="ffn",
    )(x, gain, w_gu, w_gu, w_down, w_gu_tail, w_down, out_gain)


def _norm_matmul_kernel(x_ref, gain_ref, w_ref, o_ref, h_ref):
    @pl.when(pl.program_id(1) == 0)
    def _():
        h_ref[...] = _rms_normalize(x_ref[...], gain_ref[...]).astype(BF16)

    o_ref[...] = _dot(h_ref[...], w_ref[...]).astype(o_ref.dtype)


def _norm_matmul(x, gain, w, layer, *, tm=1024, tn=2560):
    t, d = x.shape
    n_cols = w.shape[-1]
    return pl.pallas_call(
        _norm_matmul_kernel,
        grid=(t // tm, n_cols // tn),
        in_specs=[
            pl.BlockSpec((tm, d), lambda i, j: (i, 0)),
            pl.BlockSpec((None, 1, d), lambda i, j: (layer, 0, 0)),
            pl.BlockSpec((None, d, tn), lambda i, j: (layer, 0, j)),
        ],
        out_specs=pl.BlockSpec((tm, tn), lambda i, j: (i, j)),
        out_shape=jax.ShapeDtypeStruct((t, n_cols), BF16),
        scratch_shapes=[pltpu.VMEM((tm, d), BF16)],
        compiler_params=_params(("parallel", "arbitrary"), 58),
        name="in_proj",
    )(x, gain, w)


def _attn_kernel(q0, k0, v0, q1, k1, v1, q2, k2, v2, o_ref,
                 qs1, ks1, vs1, qs2, ks2, vs2, og0, og1, og2, lg0, lg1, lg2,
                 qt2, kt2, vt2, ogt2, lgt2):
    nb = ATTN_BLOCK
    row2 = lax.broadcasted_iota(jnp.int32, (nb, 2 * nb), 0)
    col2 = lax.broadcasted_iota(jnp.int32, (nb, 2 * nb), 1)
    band = (col2 >= row2) & (col2 <= row2 + nb)
    row1 = lax.broadcasted_iota(jnp.int32, (nb, nb), 0)
    col1 = lax.broadcasted_iota(jnp.int32, (nb, nb), 1)
    causal = col1 <= row1

    ones = jnp.ones((2 * nb, HEAD_DIM), BF16)

    def block(q, k, v, mask):
        s = jnp.where(mask, _dot_nt(q, k), MASK_VALUE)
        m = jnp.max(s, axis=-1, keepdims=True)
        p = jnp.exp(s - m).astype(BF16)
        ov = _dot(p, jnp.concatenate([v, ones[:v.shape[0]]], axis=1))
        o, l = ov[:, :HEAD_DIM], ov[:, HEAD_DIM:]
        return o / l, m + jnp.log(l)

    def run_group(dilation, q_ref, k_ref, v_ref, og, lg, qs=None, ks=None, vs=None, staged=None):
        r = dilation
        inner = ATTN_INNER_STRIDE
        nblk = SEQ // (r * nb)
        if r > 1:
            qs[...] = q_ref[...].astype(F32)
            ks[...] = k_ref[...].astype(F32)
            vs[...] = v_ref[...].astype(F32)
        seg = SEQ // inner
        if staged is not None:
            qt, kt, vt, ogt, lgt = staged
            for ci in range(inner):
                for nat, t in ((qs, qt), (ks, kt), (vs, vt)):
                    t[pl.ds(ci * seg, seg), :] = nat[pl.ds(ci, seg, stride=inner), :]
            qs, ks, vs, og_w, lg_w = qt, kt, vt, ogt, lgt
        else:
            og_w, lg_w = og, lg

        def rows_of(c, first_row, count):
            if r == 1:
                return pl.ds(c + first_row, count)
            if staged is None:
                return pl.ds(c + r * first_row, count, stride=r)
            outer = r // inner
            return pl.ds((c % inner) * seg + c // inner + outer * first_row, count, stride=outer)

        for c in range(r):
            for n in range(nblk):
                first = n == 0
                nk = nb if first else 2 * nb
                q_rows = rows_of(c, nb * n, nb)
                k_rows = rows_of(c, nb * n if first else nb * (n - 1), nk)
                if r == 1:
                    q, k, v = q_ref[q_rows, :], k_ref[k_rows, :], v_ref[k_rows, :]
                else:
                    q = qs[q_rows, :].astype(BF16)
                    k = ks[k_rows, :].astype(BF16)
                    v = vs[k_rows, :].astype(BF16)
                o, lse = block(q, k, v, causal if first else band)
                og_w[q_rows, :] = o
                lg_w[q_rows, :] = lse

        if staged is not None:
            for ci in range(inner):
                og[pl.ds(ci, seg, stride=inner), :] = ogt[pl.ds(ci * seg, seg), :]
                lg[pl.ds(ci, seg, stride=inner), :] = lgt[pl.ds(ci * seg, seg), :]

    run_group(DILATED_GROUPS[0][1], q0, k0, v0, og0, lg0)
    run_group(DILATED_GROUPS[1][1], q1, k1, v1, og1, lg1, qs1, ks1, vs1)
    run_group(DILATED_GROUPS[2][1], q2, k2, v2, og2, lg2, qs2, ks2, vs2,
              (qt2, kt2, vt2, ogt2, lgt2))

    tile = 256
    for t in range(SEQ // tile):
        rows = pl.ds(t * tile, tile)
        l0, l1, l2 = lg0[rows, :], lg1[rows, :], lg2[rows, :]
        mx = jnp.maximum(jnp.maximum(l0, l1), l2)
        w0, w1, w2 = jnp.exp(l0 - mx), jnp.exp(l1 - mx), jnp.exp(l2 - mx)
        num = w0 * og0[rows, :] + w1 * og1[rows, :] + w2 * og2[rows, :]
        o_ref[rows, :] = (num / (w0 + w1 + w2)).astype(BF16)


def _attention(z):
    b = z.shape[0]
    heads_per_part = ATTN_WIDTH // HEAD_DIM

    def spec(part, group):
        base = Z_ATTN // HEAD_DIM + part * heads_per_part + group * ATTN_HEADS
        return pl.BlockSpec((None, SEQ, HEAD_DIM), lambda bi, h: (bi, 0, base + h))

    in_specs = [spec(part, group) for group in range(3) for part in range(3)]
    seq_f32 = pltpu.VMEM((SEQ, HEAD_DIM), F32)
    return pl.pallas_call(
        _attn_kernel,
        grid=(b, ATTN_HEADS),
        in_specs=in_specs,
        out_specs=pl.BlockSpec((None, SEQ, HEAD_DIM), lambda bi, h: (bi, 0, h)),
        out_shape=jax.ShapeDtypeStruct((b, SEQ, GROUP_WIDTH), BF16),
        scratch_shapes=[seq_f32] * 17,
        compiler_params=_params(("parallel", "parallel"), 40),
        name="dilated_attn",
    )(*([z] * 9))


def _hgrn_masks():
    c = HGRN_CHUNK
    t = np.arange(c)[:, None]
    s = np.arange(c)[None, :]
    masks = []
    for h in HGRN_LEVELS:
        masks.append((t // (2 * h) == s // (2 * h)) & (t % (2 * h) >= h) & (s % (2 * h) < h))
    masks.append(t == s)
    return np.stack(masks).astype(np.float32)


def _hgrn_kernel(layer, zq, zf, zi, zg, lb_ref, gn_ref, mk_ref, o_ref, st_ref, b_sc):
    c = HGRN_CHUNK
    tc = zq.shape[0]

    @pl.when(pl.program_id(1) == 0)
    def _():
        st_ref[...] = jnp.zeros_like(st_ref)

    lb_all = lb_ref[...]
    e = jnp.exp(lb_all - jnp.max(lb_all, axis=0, keepdims=True))
    p = e / jnp.sum(e, axis=0, keepdims=True)
    lower = jnp.zeros((1, HGRN_WIDTH), F32)
    for i in range(1, layer + 1):
        lower = lower + p[i:i + 1, :]
    gain = gn_ref[layer:layer + 1, :]

    rowi = lax.broadcasted_iota(jnp.int32, (c, HGRN_DIM), 0)
    sub8 = rowi % 8
    odd2 = rowi % 2 == 1
    r4 = rowi % 4
    sign_bit = jnp.uint32(0x80000000)
    head_cols = [slice(hd * HGRN_DIM, (hd + 1) * HGRN_DIM) for hd in range(HGRN_HEADS)]
    head_lb = [lower[:, cols] for cols in head_cols]
    head_oml = [1.0 - lb for lb in head_lb]

    def roll_in_vreg(v, d):
        return jnp.concatenate([pltpu.roll(v[r:r + 8], d, axis=0) for r in range(0, c, 8)], axis=0)

    def chunk(ci, carry):
        rows = pl.ds(pl.multiple_of(ci * c, c), c)
        for hd in range(HGRN_HEADS):
            cols = head_cols[hd]
            lb, oml = head_lb[hd], head_oml[hd]
            q = zq[rows, cols].astype(F32)
            fl = zf[rows, cols].astype(F32)
            iv = zi[rows, cols]
            sg = 1.0 / (1.0 + jnp.exp(-fl))
            lf = jnp.log(lb + oml * sg)
            kk = oml * (1.0 - sg)
            qf = q / (1.0 + jnp.exp(-q))

            b = lf
            for d in (1, 2, 4):
                b = b + jnp.where(sub8 >= d, roll_in_vreg(b, d), 0.0)
            groups = [b[0:8]]
            for r in range(8, c, 8):
                groups.append(b[r:r + 8] + jnp.broadcast_to(groups[-1][7:8], (8, HGRN_DIM)))
            b = jnp.concatenate(groups, axis=0)
            b_sc[hd] = b
            b_last = b_sc[hd, c - 1:c, :]

            qb = qf.astype(BF16)
            kb = kk.astype(BF16)
            scores = mk_ref[len(HGRN_LEVELS)] * jnp.sum(qf * kk, axis=-1, keepdims=True)
            for lev, h in enumerate(HGRN_LEVELS):
                if h == 1:
                    g = jnp.where(odd2, roll_in_vreg(b, 1), b)
                elif h == 2:
                    g = jnp.where(r4 == 0, roll_in_vreg(b, 7),
                                  jnp.where(r4 == 1, b,
                                            jnp.where(r4 == 2, roll_in_vreg(b, 1), roll_in_vreg(b, 2))))
                else:
                    g = jnp.concatenate(
                        [jnp.broadcast_to(b_sc[hd, pl.ds(p0 + h - 1, 1), :], (2 * h, HGRN_DIM))
                         for p0 in range(0, c, 2 * h)], axis=0)
                neg_abs = lax.bitcast_convert_type(
                    lax.bitcast_convert_type(b - g, jnp.uint32) | sign_bit, F32)
                ed = jnp.exp(neg_abs).astype(BF16)
                scores = scores + _dot_nt(qb * ed, kb * ed) * mk_ref[lev]

            st = st_ref[hd]
            o = _dot_nt((qf * jnp.exp(b)).astype(BF16), st.astype(BF16))
            o = o + _dot(scores.astype(BF16), iv)
            kd = (kk * jnp.exp(b_last - b)).astype(BF16)
            st_ref[hd] = jnp.exp(b_last) * st + _dot_tn(iv, kd)

            gt = zg[rows, cols].astype(F32)
            y = _rms_normalize(o, gain) * (gt / (1.0 + jnp.exp(-gt)))
            o_ref[rows, cols] = y.astype(BF16)
        return carry

    lax.fori_loop(0, tc // c, chunk, 0)


def _hgrn(z, hgrn_lb, hgrn_norm, masks, layer, *, tc=512):
    b = z.shape[0]

    def spec(part):
        return pl.BlockSpec((None, tc, HGRN_WIDTH), lambda bi, t: (bi, t, part))

    return pl.pallas_call(
        functools.partial(_hgrn_kernel, layer),
        grid=(b, SEQ // tc),
        in_specs=[spec(0), spec(1), spec(2), spec(3),
                  pl.BlockSpec(hgrn_lb.shape, lambda bi, t: (0, 0)),
                  pl.BlockSpec(hgrn_norm.shape, lambda bi, t: (0, 0)),
                  pl.BlockSpec(masks.shape, lambda bi, t: (0, 0, 0))],
        out_specs=pl.BlockSpec((None, tc, HGRN_WIDTH), lambda bi, t: (bi, t, 0)),
        out_shape=jax.ShapeDtypeStruct((b, SEQ, HGRN_WIDTH), BF16),
        scratch_shapes=[pltpu.VMEM((HGRN_HEADS, HGRN_DIM, HGRN_DIM), F32),
                        pltpu.VMEM((HGRN_HEADS, HGRN_CHUNK, HGRN_DIM), F32)],
        compiler_params=_params(("parallel", "arbitrary"), 40),
        name="hgrn2",
    )(z, z, z, z, hgrn_lb, hgrn_norm, masks)


def _combine_kernel(x_ref, at_ref, hg_ref, gl_ref, bg_ref, wpa_ref, wpm_ref, wo_ref, o_ref, y_ref):
    d = o_ref.shape[1]
    tn = 512
    at = at_ref[...]
    hg = hg_ref[...]
    for c0 in range(0, d, tn):
        ga = jax.nn.sigmoid(gl_ref[:, c0:c0 + tn].astype(F32) + bg_ref[:, c0:c0 + tn])
        gm = jax.nn.sigmoid(gl_ref[:, d + c0:d + c0 + tn].astype(F32) + bg_ref[:, d + c0:d + c0 + tn])
        y = ga * _dot(at, wpa_ref[:, c0:c0 + tn]) + gm * _dot(hg, wpm_ref[:, c0:c0 + tn])
        y_ref[:, c0:c0 + tn] = y.astype(BF16)
    o_ref[...] = x_ref[...] + _dot(y_ref[...], wo_ref[...])


def _combine(x, attn, hgrn, z, b_gate, w_pa, w_pm, w_o, layer, *, tm=512):
    t, d = x.shape

    def rows(width, col_block=0):
        return pl.BlockSpec((tm, width), lambda i: (i, col_block))

    def whole(a):
        return pl.BlockSpec((None,) + a.shape[1:], lambda i: (layer,) + (0,) * (a.ndim - 1),
                            pipeline_mode=pl.Buffered(1))

    return pl.pallas_call(
        _combine_kernel,
        grid=(t // tm,),
        in_specs=[rows(d), rows(GROUP_WIDTH), rows(HGRN_WIDTH), rows(2 * d, Z_GATE // (2 * d)),
                  whole(b_gate), whole(w_pa), whole(w_pm), whole(w_o)],
        out_specs=rows(d),
        out_shape=jax.ShapeDtypeStruct((t, d), F32),
        scratch_shapes=[pltpu.VMEM((tm, d), BF16)],
        compiler_params=_params(("parallel",), 56),
        name="merge_out_proj",
    )(x, attn, hgrn, z, b_gate, w_pa, w_pm, w_o)


def kernel(x, ffn_norm, ffn_w_in, ffn_w_out, mix_norm, w_in, b_gate, hgrn_lb, hgrn_norm,
           w_proj_attn, w_proj_hgrn, w_out, final_norm):
    bsz, seq, d = x.shape
    assert seq == SEQ and d == D_MODEL
    t = bsz * seq

    w_gu, w_gu_tail = _split_ffn_in(ffn_w_in)
    w_down = ffn_w_out
    w_mix = _reorder_mix_in(w_in)
    w_pa = w_proj_attn.astype(BF16)
    w_pm = w_proj_hgrn.astype(BF16)
    w_o = w_out.astype(BF16)
    ffn_gain = ffn_norm.reshape(DEPTH, 2, 1, d)
    mix_gain = mix_norm.reshape(DEPTH, 1, d)
    gate_bias = b_gate.reshape(DEPTH, 1, 2 * d)
    masks = jnp.asarray(_hgrn_masks())

    out_gain = final_norm.reshape(1, d)
    xf = x.reshape(t, d)
    for l in range(DEPTH):
        xf = _ffn(xf, ffn_gain, w_gu, w_gu_tail, w_down, out_gain, l, 0)
        z = _norm_matmul(xf, mix_gain, w_mix, l)
        z3 = z.reshape(bsz, seq, Z_WIDTH)
        attn = _attention(z3)
        hgrn = _hgrn(z3, hgrn_lb, hgrn_norm, masks, l)
        xf = _combine(xf, attn.reshape(t, GROUP_WIDTH), hgrn.reshape(t, HGRN_WIDTH), z,
                      gate_bias, w_pa, w_pm, w_o, l)
        xf = _ffn(xf, ffn_gain, w_gu, w_gu_tail, w_down, out_gain, l, 1, norm_output=l == DEPTH - 1)
    return xf.reshape(bsz, seq, d)
```
